```python
import math, functools
import jax, jax.numpy as jnp
from jax import lax
import numpy as np

D_MODEL = 2048
BATCH = 1
SEQ = 8192
DEPTH = 2
DEC_BATCH = 32
DEC_SEQ = 4
PAST_LEN = 8192
PAGE_SIZE = 128

META_LEN = 16
GLA_HEADS = 4
GLA_DK = 64
GLA_DV = 128
GLA_RANK = 16
GLA_TAU = 16.0
GLA_CHUNK = 64
DIFF_HEADS = 6
DIFF_DH = 64
DIFF_DV = 2 * DIFF_DH
SB_HEADS = 6
SB_DH = 128
Q_BLOCK = 128
GLA_WIDTH = GLA_HEADS * GLA_DV
DIFF_WIDTH = DIFF_HEADS * DIFF_DV
SB_WIDTH = SB_HEADS * SB_DH
D_MIX = GLA_WIDTH + DIFF_WIDTH + SB_WIDTH
IN_WIDTHS = (GLA_HEADS * GLA_DK, GLA_HEADS * GLA_DK, GLA_WIDTH, GLA_RANK, GLA_WIDTH,
             DIFF_HEADS * 2 * DIFF_DH, DIFF_HEADS * 2 * DIFF_DH, DIFF_WIDTH, DIFF_WIDTH,
             SB_WIDTH, SB_WIDTH, SB_WIDTH, SB_WIDTH)
D_IN = sum(IN_WIDTHS)
EPS = 1e-6

kernel_name = "hymba_gla_diff_stickbreak_step"


def rmsnorm(x, g):
    xf = x.astype(jnp.float32)
    y = xf * lax.rsqrt(jnp.mean(xf * xf, axis=-1, keepdims=True) + EPS)
    return (y * g.astype(jnp.float32)).astype(x.dtype)


def split_columns(p):
    outs, off = [], 0
    for w in IN_WIDTHS:
        outs.append(p[..., off:off + w])
        off += w
    return outs


def branch_inputs(h, w_in, gla_w_gate, gla_b_gate):
    B, T, _ = h.shape
    p = jnp.einsum('btd,dn->btn', h, w_in)
    gq, gk, gv, gr, gg, dq, dk, dv, dg, sq, sk, sv, sg = split_columns(p)
    heads = lambda a, n: a.reshape(B, T, n, -1)
    log_g = jax.nn.log_sigmoid((jnp.einsum('btr,rk->btk', gr, gla_w_gate) + gla_b_gate).astype(jnp.float32)) / GLA_TAU
    gla = (heads(gq, GLA_HEADS) * GLA_DK ** -0.5, heads(gk, GLA_HEADS), heads(gv, GLA_HEADS), heads(log_g, GLA_HEADS))
    diff = (heads(dq, DIFF_HEADS), heads(dk, DIFF_HEADS), heads(dv, DIFF_HEADS))
    sb = (heads(sq, SB_HEADS), heads(sk, SB_HEADS), heads(sv, SB_HEADS))
    return gla, diff, sb, (gg, dg, sg)


def branch_merge(o_gla, o_diff, o_sb, gates, gla_norm, diff_norm, w_out, lam_init):
    B, T = o_gla.shape[:2]
    gg, dg, sg = gates
    o_gla = rmsnorm(o_gla, gla_norm).reshape(B, T, GLA_WIDTH)
    o_diff = (rmsnorm(o_diff, diff_norm) * (1.0 - lam_init)).reshape(B, T, DIFF_WIDTH)
    o_sb = o_sb.reshape(B, T, SB_WIDTH)
    o = jnp.concatenate([o_gla * jax.nn.silu(gg), o_diff * jax.nn.silu(dg), o_sb * jax.nn.silu(sg)], axis=-1)
    return jnp.einsum('btm,md->btd', o, w_out)


def gla_chunked(q, k, v, log_g, s0, chunk):
    B, T, H, DK = q.shape
    DV = v.shape[-1]
    n = T // chunk
    f32 = jnp.float32

    def to_chunks(a):
        return a.astype(f32).reshape(B, n, chunk, H, a.shape[-1]).transpose(1, 0, 3, 2, 4)

    causal = jnp.tril(jnp.ones((chunk, chunk), bool))[:, :, None]

    def step(s, inp):
        qc, kc, vc, gc = inp
        b = jnp.cumsum(gc, axis=2)
        rel = jnp.where(causal, b[:, :, :, None, :] - b[:, :, None, :, :], -jnp.inf)
        att = jnp.einsum('bhid,bhjd,bhijd->bhij', qc, kc, jnp.exp(rel))
        o = jnp.einsum('bhid,bhde->bhie', qc * jnp.exp(b), s) + jnp.einsum('bhij,bhje->bhie', att, vc)
        b_end = b[:, :, -1:, :]
        s_new = jnp.exp(b_end[:, :, 0, :, None]) * s + jnp.einsum('bhjd,bhje->bhde', kc * jnp.exp(b_end - b), vc)
        return s_new, o

    s_fin, o = lax.scan(step, s0.astype(f32), (to_chunks(q), to_chunks(k), to_chunks(v), to_chunks(log_g)))
    o = o.transpose(1, 0, 3, 2, 4).reshape(B, T, H, DV).astype(v.dtype)
    return s_fin, o


def diff_lambda(lq1, lk1, lq2, lk2, lam_init):
    f = jnp.float32
    return (jnp.exp(jnp.sum(lq1.astype(f) * lk1.astype(f)))
            - jnp.exp(jnp.sum(lq2.astype(f) * lk2.astype(f))) + lam_init)


def diff_attend(q, q_pos, k, v, k_pos, lam):
    B, Tq, H, _ = q.shape
    Tk = k.shape[1]
    q2 = q.reshape(B, Tq, H, 2, DIFF_DH)
    k2 = k.reshape(B, Tk, H, 2, DIFF_DH)
    s = jnp.einsum('bqhmd,bkhmd->bmhqk', q2, k2).astype(jnp.float32) * DIFF_DH ** -0.5
    s = jnp.where(k_pos[None, :] <= q_pos[:, None], s, -jnp.inf)
    p = jax.nn.softmax(s, axis=-1)
    w = p[:, 0] - lam * p[:, 1]
    return jnp.einsum('bhqk,bkhd->bqhd', w.astype(v.dtype), v)


def sb_attend(q, q_pos, k, v, k_pos):
    z = jnp.einsum('bqhd,bkhd->bhqk', q, k).astype(jnp.float32) * SB_DH ** -0.5
    visible = k_pos[None, :] < q_pos[:, None]
    log_beta = jax.nn.log_sigmoid(z)
    log_keep = jnp.where(visible, jax.nn.log_sigmoid(-z), 0.0)
    tail = lax.cumsum(log_keep, axis=3, reverse=True) - log_keep
    a = jnp.where(visible, jnp.exp(log_beta + tail), 0.0)
    return jnp.einsum('bhqk,bkhd->bqhd', a.astype(v.dtype), v)


def sweep_query_blocks(fn, q, q_pos, k, v, k_pos):
    B, T, H, d = q.shape
    nb = T // Q_BLOCK
    qb = jnp.moveaxis(q.reshape(B, nb, Q_BLOCK, H, d), 1, 0)
    pb = q_pos.reshape(nb, Q_BLOCK)
    ob = lax.map(lambda qp: fn(qp[0], qp[1], k, v, k_pos), (qb, pb))
    return jnp.moveaxis(ob, 0, 1).reshape(B, T, H, ob.shape[-1])


def prompt_attention(fn, q, k, v):
    pos = jnp.arange(q.shape[1])
    o_meta = fn(q[:, :META_LEN], pos[:META_LEN], k[:, :META_LEN], v[:, :META_LEN], pos[:META_LEN])
    o_real = sweep_query_blocks(fn, q[:, META_LEN:], pos[META_LEN:], k, v, pos)
    return jnp.concatenate([o_meta, o_real], axis=1)


def prompt_layer(x, lp, lam_init):
    norm, w_in, gla_w_gate, gla_b_gate, gla_norm, lq1, lk1, lq2, lk2, diff_norm, w_out = lp
    h = rmsnorm(x, norm)
    (gq, gk, gv, lg), (dq, dk, dv), (sq, sk, sv), gates = branch_inputs(h, w_in, gla_w_gate, gla_b_gate)
    B = x.shape[0]
    s0 = jnp.zeros((B, GLA_HEADS, GLA_DK, GLA_DV), jnp.float32)
    m = META_LEN
    s_meta, o_gla_meta = gla_chunked(gq[:, :m], gk[:, :m], gv[:, :m], lg[:, :m], s0, META_LEN)
    s_fin, o_gla_real = gla_chunked(gq[:, m:], gk[:, m:], gv[:, m:], lg[:, m:], s_meta, GLA_CHUNK)
    o_gla = jnp.concatenate([o_gla_meta, o_gla_real], axis=1)
    lam = diff_lambda(lq1, lk1, lq2, lk2, lam_init)
    o_diff = prompt_attention(functools.partial(diff_attend, lam=lam), dq, dk, dv)
    o_sb = prompt_attention(sb_attend, sq, sk, sv)
    out = branch_merge(o_gla, o_diff, o_sb, gates, gla_norm, diff_norm, w_out, lam_init)
    return x + out, (dk, dv, sk, sv, s_fin.astype(x.dtype))


def gather_pages(pool, layer, page_table):
    g = pool[layer, page_table]
    return g.reshape(g.shape[0], g.shape[1] * g.shape[2], g.shape[3], g.shape[4])


def sample_layer(x, lp, lam_init, layer, cache_diff_k, cache_diff_v, cache_sb_k, cache_sb_v, state_gla, page_table):
    norm, w_in, gla_w_gate, gla_b_gate, gla_norm, lq1, lk1, lq2, lk2, diff_norm, w_out = lp
    h = rmsnorm(x, norm)
    (gq, gk, gv, lg), (dq, dk, dv), (sq, sk, sv), gates = branch_inputs(h, w_in, gla_w_gate, gla_b_gate)
    T = x.shape[1]
    past_len = page_table.shape[1] * cache_diff_k.shape[2]
    q_pos = past_len + jnp.arange(T)
    k_pos = jnp.arange(past_len + T)
    s_new, o_gla = gla_chunked(gq, gk, gv, lg, state_gla[layer], T)
    lam = diff_lambda(lq1, lk1, lq2, lk2, lam_init)
    dk_all = jnp.concatenate([gather_pages(cache_diff_k, layer, page_table), dk], axis=1)
    dv_all = jnp.concatenate([gather_pages(cache_diff_v, layer, page_table), dv], axis=1)
    o_diff = diff_attend(dq, q_pos, dk_all, dv_all, k_pos, lam)
    sk_all = jnp.concatenate([gather_pages(cache_sb_k, layer, page_table), sk], axis=1)
    sv_all = jnp.concatenate([gather_pages(cache_sb_v, layer, page_table), sv], axis=1)
    o_sb = sb_attend(sq, q_pos, sk_all, sv_all, k_pos)
    out = branch_merge(o_gla, o_diff, o_sb, gates, gla_norm, diff_norm, w_out, lam_init)
    return x + out, (dk, dv, sk, sv, s_new.astype(state_gla.dtype))


def setup_inputs(seed: int = 0) -> dict:
    key = jax.random.key(seed)
    ks = jax.random.split(key, 24)
    f = jnp.float32
    n_pages = PAST_LEN // PAGE_SIZE
    n_used = DEC_BATCH * n_pages
    n_phys = n_used + max(1, n_used // 4)
    nrm = lambda k, shape, s=1.0: jax.random.normal(k, shape, f) * s
    page_table = jax.random.permutation(ks[7], n_phys)[:n_used].reshape(DEC_BATCH, n_pages).astype(jnp.int32)
    return {
        "x_prompt": nrm(ks[0], (BATCH, SEQ, D_MODEL)),
        "x_sample": nrm(ks[1], (DEC_BATCH, DEC_SEQ, D_MODEL)),
        "cache_diff_k": nrm(ks[2], (DEPTH, n_phys, PAGE_SIZE, DIFF_HEADS, 2 * DIFF_DH)),
        "cache_diff_v": nrm(ks[3], (DEPTH, n_phys, PAGE_SIZE, DIFF_HEADS, DIFF_DV)),
        "cache_sb_k": nrm(ks[4], (DEPTH, n_phys, PAGE_SIZE, SB_HEADS, SB_DH)),
        "cache_sb_v": nrm(ks[5], (DEPTH, n_phys, PAGE_SIZE, SB_HEADS, SB_DH)),
        "state_gla": nrm(ks[6], (DEPTH, DEC_BATCH, GLA_HEADS, GLA_DK, GLA_DV)),
        "page_table": page_table,
        "meta_tokens": nrm(ks[8], (META_LEN, D_MODEL)),
        "norm_mix": 1.0 + nrm(ks[9], (DEPTH, D_MODEL), 0.02),
        "w_in": nrm(ks[10], (DEPTH, D_MODEL, D_IN), D_MODEL ** -0.5),
        "gla_w_gate": nrm(ks[11], (DEPTH, GLA_RANK, GLA_HEADS * GLA_DK), GLA_RANK ** -0.5),
        "gla_b_gate": nrm(ks[12], (DEPTH, GLA_HEADS * GLA_DK), 0.1),
        "gla_norm": 1.0 + nrm(ks[13], (DEPTH, GLA_DV), 0.02),
        "diff_lambda_q1": nrm(ks[14], (DEPTH, DIFF_DH), 0.1),
        "diff_lambda_k1": nrm(ks[15], (DEPTH, DIFF_DH), 0.1),
        "diff_lambda_q2": nrm(ks[16], (DEPTH, DIFF_DH), 0.1),
        "diff_lambda_k2": nrm(ks[17], (DEPTH, DIFF_DH), 0.1),
        "diff_norm": 1.0 + nrm(ks[18], (DEPTH, DIFF_DV), 0.02),
        "w_out": nrm(ks[19], (DEPTH, D_MIX, D_MODEL), D_MIX ** -0.5),
        "final_norm": 1.0 + nrm(ks[20], (D_MODEL,), 0.02),
    }


def reference(x_prompt, x_sample, cache_diff_k, cache_diff_v, cache_sb_k, cache_sb_v, state_gla, page_table,
              meta_tokens, norm_mix, w_in, gla_w_gate, gla_b_gate, gla_norm,
              diff_lambda_q1, diff_lambda_k1, diff_lambda_q2, diff_lambda_k2, diff_norm, w_out, final_norm):
    B = x_prompt.shape[0]
    meta = jnp.broadcast_to(meta_tokens[None].astype(x_prompt.dtype), (B, META_LEN, D_MODEL))
    xp = jnp.concatenate([meta, x_prompt], axis=1)
    xs = x_sample
    p_rows, s_rows = [], []
    for l in range(DEPTH):
        lam_init = 0.8 - 0.6 * math.exp(-0.3 * l)
        lp = (norm_mix[l], w_in[l], gla_w_gate[l], gla_b_gate[l], gla_norm[l],
              diff_lambda_q1[l], diff_lambda_k1[l], diff_lambda_q2[l], diff_lambda_k2[l], diff_norm[l], w_out[l])
        xp, pr = prompt_layer(xp, lp, lam_init)
        xs, sr = sample_layer(xs, lp, lam_init, l, cache_diff_k, cache_diff_v, cache_sb_k, cache_sb_v,
                              state_gla, page_table)
        p_rows.append(pr)
        s_rows.append(sr)
    y_prompt = rmsnorm(xp[:, META_LEN:], final_norm)
    y_sample = rmsnorm(xs, final_norm)
    stack = lambda rows, i: jnp.stack([r[i] for r in rows])
    p_diff_k, p_diff_v, p_sb_k, p_sb_v, p_gla = (stack(p_rows, 0), stack(p_rows, 1), stack(p_rows, 2),
                                                 stack(p_rows, 3), stack(p_rows, 4))
    s_diff_k, s_diff_v, s_sb_k, s_sb_v, s_gla = (stack(s_rows, 0), stack(s_rows, 1), stack(s_rows, 2),
                                                 stack(s_rows, 3), stack(s_rows, 4))
    return (y_prompt, y_sample, p_diff_k, p_diff_v, p_sb_k, p_sb_v, p_gla,
            s_diff_k, s_diff_v, s_sb_k, s_sb_v, s_gla)
```

```python
import functools
import math

import jax
import jax.numpy as jnp
from jax import lax
from jax.experimental import pallas as pl
from jax.experimental.pallas import tpu as pltpu

F32 = jnp.float32
BF16 = jnp.bfloat16

META_LEN = 16
GLA_HEADS = 4
GLA_DK = 64
GLA_DV = 128
GLA_RANK = 16
GLA_TAU = 16.0
DIFF_HEADS = 6
DIFF_DH = 64
SB_HEADS = 6
SB_DH = 128
HEAD_W = 128
EPS = 1e-6
NEG = -1e30

GLA_WIDTH = GLA_HEADS * GLA_DV
ATT_WIDTH = DIFF_HEADS * HEAD_W
IN_WIDTHS = (256, 256, 512, 16, 512, 768, 768, 768, 768, 768, 768, 768, 768)

COL_TILE = 256
PG_WIDTH = 7 * COL_TILE
PG_Q, PG_K, PG_V, PG_G, PG_R = 0, 256, 512, 1024, 1536
SEGMENTS = (("pg", 7), ("dq", 3), ("dk", 3), ("dv", 3), ("dg", 3), ("sq", 3), ("sk", 3), ("sv", 3), ("sg", 3))
N_COL_TILES = sum(n for _, n in SEGMENTS)

VMEM_LIMIT = 56 * 1024 * 1024


def _cparams(sem):
    return pltpu.CompilerParams(dimension_semantics=sem, vmem_limit_bytes=VMEM_LIMIT)


def _softplus(z):
    return jnp.maximum(z, 0.0) + jnp.log1p(jnp.exp(-jnp.abs(z)))


def _log_sigmoid(z):
    return -_softplus(-z)


def _silu(g):
    return g / (1.0 + jnp.exp(-g))


def _nt_dot(a, b):
    return lax.dot_general(a, b, (((1,), (1,)), ((), ())), preferred_element_type=F32)


def _dot(a, b):
    return jnp.dot(a, b, preferred_element_type=F32)


def _rms(o, g):
    return o * lax.rsqrt(jnp.mean(o * o, axis=-1, keepdims=True) + EPS) * g


def _pack_w_in(w):
    offs = [0]
    for wd in IN_WIDTHS:
        offs.append(offs[-1] + wd)
    c = [w[:, offs[i]:offs[i + 1]] for i in range(len(IN_WIDTHS))]
    gq, gk, gv, gr, gg = c[:5]
    pad = jnp.zeros((w.shape[0], PG_WIDTH - PG_R - GLA_RANK), w.dtype)
    return jnp.concatenate([gq, gk, gv, gg, gr, pad] + c[5:], axis=1).astype(BF16)


def _inproj_kernel(x_ref, g_ref, w_ref, *refs, t_valid, tm, outs):
    out_refs = refs[:len(outs)]
    h_ref = refs[len(outs)]
    i = pl.program_id(0)
    j = pl.program_id(1)

    @pl.when(j == 0)
    def _():
        x = x_ref[...]
        y = x * lax.rsqrt(jnp.mean(x * x, axis=-1, keepdims=True) + EPS) * g_ref[...]
        rows = i * tm + lax.broadcasted_iota(jnp.int32, (tm, 1), 0)
        h_ref[...] = jnp.where(rows < t_valid, y, 0.0).astype(BF16)

    acc = _dot(h_ref[...], w_ref[...])

    for o_ref, (start, n, head_major) in zip(out_refs, outs):
        @pl.when((j >= start) & (j < start + n))
        def _(o_ref=o_ref, head_major=head_major):
            if head_major:
                o_ref[0] = acc[:, :HEAD_W].astype(o_ref.dtype)
                o_ref[1] = acc[:, HEAD_W:].astype(o_ref.dtype)
            else:
                o_ref[...] = acc.astype(o_ref.dtype)


def _inproj(x, g, w_packed, *, t_pad, tm):
    t_valid, d = x.shape
    assert t_pad % tm == 0
    starts = {}
    s = 0
    for name, n in SEGMENTS:
        starts[name] = (s, n)
        s += n
    spec = [("pg", "pg", False, F32, t_pad)]
    for a in ("d", "s"):
        spec += [(a + "q", a + "q", True, BF16, t_pad),
                 (a + "k", a + "k", True, F32, t_valid), (a + "kb", a + "k", True, BF16, t_pad),
                 (a + "v", a + "v", True, F32, t_valid), (a + "vb", a + "v", True, BF16, t_pad),
                 (a + "g", a + "g", False, F32, t_pad)]
    out_shapes, out_specs, outs = [], [], []
    for key, seg, head_major, dt, rows in spec:
        start, n = starts[seg]
        outs.append((start, n, head_major))
        if head_major:
            out_shapes.append(jax.ShapeDtypeStruct((2 * n, rows, HEAD_W), dt))
            out_specs.append(pl.BlockSpec(
                (2, tm, HEAD_W), lambda i, j, start=start, n=n: (jnp.clip(j - start, 0, n - 1), i, 0)))
        else:
            out_shapes.append(jax.ShapeDtypeStruct((rows, n * COL_TILE), dt))
            out_specs.append(pl.BlockSpec(
                (tm, COL_TILE), lambda i, j, start=start, n=n: (i, jnp.clip(j - start, 0, n - 1))))
    res = pl.pallas_call(
        functools.partial(_inproj_kernel, t_valid=t_valid, tm=tm, outs=tuple(outs)),
        out_shape=out_shapes,
        grid=(t_pad // tm, N_COL_TILES),
        in_specs=[pl.BlockSpec((tm, d), lambda i, j: (i, 0)),
                  pl.BlockSpec((1, d), lambda i, j: (0, 0)),
                  pl.BlockSpec((d, COL_TILE), lambda i, j: (0, j))],
        out_specs=out_specs,
        scratch_shapes=[pltpu.VMEM((tm, d), BF16)],
        compiler_params=_cparams(("arbitrary", "arbitrary")),
        name="inproj",
    )(x, g.reshape(1, d), w_packed)
    return {key: r for (key, *_), r in zip(spec, res)}


def _cumsum_rows(lg, n):
    ri = lax.broadcasted_iota(jnp.int32, (n, n), 0)
    ci = lax.broadcasted_iota(jnp.int32, (n, n), 1)
    tri = jnp.where(ri >= ci, 1.0, 0.0).astype(BF16)
    hi = lg.astype(BF16)
    lo = (lg - hi.astype(F32)).astype(BF16)
    return _dot(tri, hi) + _dot(tri, lo)


def _gla_prompt_kernel(pg_ref, wg_ref, bg_ref, gn_ref, og_ref, sfin_ref, s_ref, *, t_valid, chunk, sub):
    step = pl.program_id(0)

    @pl.when(step == 0)
    def _():
        s_ref[...] = jnp.zeros_like(s_ref)

    rows = step * chunk + lax.broadcasted_iota(jnp.int32, (chunk, 1), 0)
    x = _dot(pg_ref[:, PG_R:PG_R + COL_TILE].astype(BF16), wg_ref[...]) + bg_ref[...]
    lg = jnp.where(rows < t_valid, _log_sigmoid(x) * (1.0 / GLA_TAU), 0.0)
    b = _cumsum_rows(lg, chunk)

    lane = lax.broadcasted_iota(jnp.int32, (1, HEAD_W), 1)
    head_mask = (lane < GLA_DK, lane >= GLA_DK)
    n_sub = chunk // sub
    ri = lax.broadcasted_iota(jnp.int32, (sub, chunk), 0)
    ci = lax.broadcasted_iota(jnp.int32, (sub, chunk), 1)
    krow = lax.broadcasted_iota(jnp.int32, (chunk, 1), 0)

    for p in range(2):
        qp = pg_ref[:, PG_Q + HEAD_W * p:PG_Q + HEAD_W * (p + 1)] * (GLA_DK ** -0.5)
        kp = pg_ref[:, PG_K + HEAD_W * p:PG_K + HEAD_W * (p + 1)]
        bp = b[:, HEAD_W * p:HEAD_W * (p + 1)]
        s_old = s_ref[p]
        s_bf = s_old.astype(BF16)
        qe = qp * jnp.exp(bp)
        b_end = bp[chunk - 1:chunk]
        kend_t = (kp * jnp.exp(b_end - bp)).T
        dec_t = jnp.broadcast_to(jnp.exp(b_end), (HEAD_W, HEAD_W)).T
        q_blk, k_blk = [], []
        for blk in range(n_sub):
            lo, hi = blk * sub, (blk + 1) * sub
            r = jnp.zeros((1, HEAD_W), F32) if blk == 0 else bp[lo - 1:lo]
            q_blk.append(qp[lo:hi] * jnp.exp(bp[lo:hi] - r))
            k_blk.append((kp * jnp.exp(jnp.where(krow < hi, r - bp, 0.0))).astype(BF16))
        upd = []
        for hh in range(2):
            h = 2 * p + hh
            vb = pg_ref[:, PG_V + GLA_DV * h:PG_V + GLA_DV * (h + 1)].astype(BF16)
            o_inter = _dot(jnp.where(head_mask[hh], qe, 0.0).astype(BF16), s_bf)
            o_rows = []
            for blk in range(n_sub):
                lo, hi = blk * sub, (blk + 1) * sub
                att = _nt_dot(jnp.where(head_mask[hh], q_blk[blk], 0.0).astype(BF16), k_blk[blk])
                att = jnp.where(ci <= ri + lo, att, 0.0)
                o_rows.append(_dot(att.astype(BF16), vb))
            o = o_inter + jnp.concatenate(o_rows, axis=0)
            og_ref[:, GLA_DV * h:GLA_DV * (h + 1)] = _rms(o, gn_ref[...])
            upd.append(_dot(kend_t[GLA_DK * hh:GLA_DK * (hh + 1)].astype(BF16), vb))
        s_ref[p] = dec_t * s_old + jnp.concatenate(upd, axis=0)

    @pl.when(step == pl.num_programs(0) - 1)
    def _():
        sfin_ref[...] = s_ref[...]


def _gla_prompt(pg, wg_pad, bg, gn, *, t_valid, chunk=128, sub=16):
    t_pad = pg.shape[0]
    n_steps = pl.cdiv(t_valid, chunk)
    assert n_steps * chunk <= t_pad
    og, sfin = pl.pallas_call(
        functools.partial(_gla_prompt_kernel, t_valid=t_valid, chunk=chunk, sub=sub),
        out_shape=[jax.ShapeDtypeStruct((t_pad, GLA_WIDTH), F32),
                   jax.ShapeDtypeStruct((2, HEAD_W, GLA_DV), F32)],
        grid=(n_steps,),
        in_specs=[pl.BlockSpec((chunk, PG_WIDTH), lambda i: (i, 0)),
                  pl.BlockSpec((COL_TILE, COL_TILE), lambda i: (0, 0)),
                  pl.BlockSpec((1, COL_TILE), lambda i: (0, 0)),
                  pl.BlockSpec((1, GLA_DV), lambda i: (0, 0))],
        out_specs=[pl.BlockSpec((chunk, GLA_WIDTH), lambda i: (i, 0)),
                   pl.BlockSpec((2, HEAD_W, GLA_DV), lambda i: (0, 0, 0))],
        scratch_shapes=[pltpu.VMEM((2, HEAD_W, GLA_DV), F32)],
        compiler_params=_cparams(("arbitrary",)),
        name="gla_prompt",
    )(pg, wg_pad, bg, gn)
    return og, sfin.reshape(GLA_HEADS, GLA_DK, GLA_DV)


def _gla_sample_kernel(pg_ref, st_ref, wg_ref, bg_ref, gn_ref, og_ref, snew_ref, *, n_tok):
    pad = HEAD_W - n_tok
    zpad = jnp.zeros((pad, HEAD_W), F32)
    gr = jnp.concatenate([pg_ref[0, :,PG_R:PG_R + COL_TILE], jnp.zeros((pad, COL_TILE), F32)], axis=0)
    x = _dot(gr.astype(BF16), wg_ref[...]) + bg_ref[...]
    rows = lax.broadcasted_iota(jnp.int32, (HEAD_W, 1), 0)
    lg = jnp.where(rows < n_tok, _log_sigmoid(x) * (1.0 / GLA_TAU), 0.0)
    b = _cumsum_rows(lg, HEAD_W)

    lane = lax.broadcasted_iota(jnp.int32, (1, HEAD_W), 1)
    head_mask = (lane < GLA_DK, lane >= GLA_DK)
    ri = lax.broadcasted_iota(jnp.int32, (HEAD_W, HEAD_W), 0)
    ci = lax.broadcasted_iota(jnp.int32, (HEAD_W, HEAD_W), 1)
    og = []
    for p in range(2):
        qp = jnp.concatenate([pg_ref[0, :,PG_Q + HEAD_W * p:PG_Q + HEAD_W * (p + 1)], zpad], axis=0) * (GLA_DK ** -0.5)
        kp = jnp.concatenate([pg_ref[0, :,PG_K + HEAD_W * p:PG_K + HEAD_W * (p + 1)], zpad], axis=0)
        bp = b[:, HEAD_W * p:HEAD_W * (p + 1)]
        s_old = jnp.concatenate([st_ref[0, 0, 2 * p], st_ref[0, 0, 2 * p + 1]], axis=0)
        s_bf = s_old.astype(BF16)
        qe = qp * jnp.exp(bp)
        kinv = (kp * jnp.exp(-bp)).astype(BF16)
        b_end = bp[HEAD_W - 1:HEAD_W]
        kend_t = (kp * jnp.exp(b_end - bp)).T
        dec_t = jnp.broadcast_to(jnp.exp(b_end), (HEAD_W, HEAD_W)).T
        upd = []
        for hh in range(2):
            h = 2 * p + hh
            vb = jnp.concatenate([pg_ref[0, :,PG_V + GLA_DV * h:PG_V + GLA_DV * (h + 1)], zpad], axis=0).astype(BF16)
            qm = jnp.where(head_mask[hh], qe, 0.0).astype(BF16)
            att = jnp.where(ci <= ri, _nt_dot(qm, kinv), 0.0)
            o = _dot(qm, s_bf) + _dot(att.astype(BF16), vb)
            og.append(_rms(o, gn_ref[...])[:n_tok])
            upd.append(_dot(kend_t[GLA_DK * hh:GLA_DK * (hh + 1)].astype(BF16), vb))
        s_new = dec_t * s_old + jnp.concatenate(upd, axis=0)
        snew_ref[0, 2 * p] = s_new[:GLA_DK]
        snew_ref[0, 2 * p + 1] = s_new[GLA_DK:]
    og_ref[0] = jnp.concatenate(og, axis=1)


def _gla_sample(pg, state, layer, wg_pad, bg, gn, *, n_seq, n_tok):
    pg3 = pg.reshape(n_seq, n_tok, PG_WIDTH)
    og, snew = pl.pallas_call(
        functools.partial(_gla_sample_kernel, n_tok=n_tok),
        out_shape=[jax.ShapeDtypeStruct((n_seq, n_tok, GLA_WIDTH), F32),
                   jax.ShapeDtypeStruct((n_seq, GLA_HEADS, GLA_DK, GLA_DV), F32)],
        grid=(n_seq,),
        in_specs=[pl.BlockSpec((1, n_tok, PG_WIDTH), lambda i: (i, 0, 0)),
                  pl.BlockSpec((1, 1, GLA_HEADS, GLA_DK, GLA_DV), lambda i: (layer, i, 0, 0, 0)),
                  pl.BlockSpec((COL_TILE, COL_TILE), lambda i: (0, 0)),
                  pl.BlockSpec((1, COL_TILE), lambda i: (0, 0)),
                  pl.BlockSpec((1, GLA_DV), lambda i: (0, 0))],
        out_specs=[pl.BlockSpec((1, n_tok, GLA_WIDTH), lambda i: (i, 0, 0)),
                   pl.BlockSpec((1, GLA_HEADS, GLA_DK, GLA_DV), lambda i: (i, 0, 0, 0))],
        compiler_params=_cparams(("arbitrary",)),
        name="gla_sample",
    )(pg3, state, wg_pad, bg, gn)
    return og.reshape(n_seq * n_tok, GLA_WIDTH), snew


def _diff_lambda(lqk_ref, lam_init):
    s1 = jnp.sum(lqk_ref[0:1, :] * lqk_ref[1:2, :], axis=-1, keepdims=True)
    s2 = jnp.sum(lqk_ref[2:3, :] * lqk_ref[3:4, :], axis=-1, keepdims=True)
    return jnp.exp(s1) - jnp.exp(s2) + lam_init


def _split_maps(q):
    lane = lax.broadcasted_iota(jnp.int32, q.shape, 1)
    zero = jnp.zeros_like(q)
    return jnp.concatenate([jnp.where(lane < DIFF_DH, q, zero), jnp.where(lane >= DIFF_DH, q, zero)], axis=0)


def _diff_prompt_kernel(lqk_ref, dn_ref, q_ref, k_ref, v_ref, o_ref, m_ref, l_ref, acc_ref, *, bq, bk, lam_init):
    i = pl.program_id(1)
    qq = _split_maps(q_ref[0]) * (DIFF_DH ** -0.5)
    row = i * bq + lax.broadcasted_iota(jnp.int32, (bq, 1), 0)
    rows2 = jnp.concatenate([row, row], axis=0)
    m_ref[...] = jnp.full_like(m_ref, NEG)
    l_ref[...] = jnp.zeros_like(l_ref)
    acc_ref[...] = jnp.zeros_like(acc_ref)

    def chunk(j, masked):
        off = pl.multiple_of(j * bk, bk)
        k = k_ref[0, pl.ds(off, bk), :]
        v = v_ref[0, pl.ds(off, bk), :]
        s = _nt_dot(qq, k)
        if masked:
            col = j * bk + lax.broadcasted_iota(jnp.int32, (1, bk), 1)
            s = jnp.where(col <= rows2, s, NEG)
        m_old = m_ref[...]
        m_new = jnp.maximum(m_old, jnp.max(s, axis=-1, keepdims=True))
        alpha = jnp.exp(m_old - m_new)
        p = jnp.exp(s - m_new)
        l_ref[...] = alpha * l_ref[...] + jnp.sum(p, axis=-1, keepdims=True)
        acc_ref[...] = alpha * acc_ref[...] + _dot(p.astype(BF16), v)
        m_ref[...] = m_new

    n_full = (i * bq + 1) // bk
    n_need = (i * bq + bq + bk - 1) // bk

    def full_body(j, c):
        chunk(j, False)
        return c

    def diag_body(j, c):
        chunk(j, True)
        return c

    lax.fori_loop(0, n_full, full_body, 0)
    lax.fori_loop(n_full, n_need, diag_body, 0)

    lam = _diff_lambda(lqk_ref, lam_init)
    on = acc_ref[...] / l_ref[...]
    o = on[:bq] - lam * on[bq:]
    o_ref[...] = _rms(o, dn_ref[...]) * (1.0 - lam_init)


def _diff_prompt(q, k, v, lqk, dn, *, t_valid, lam_init, bq=256, bk=512):
    n_h, t_pad, _ = q.shape
    nq = pl.cdiv(t_valid, bq)
    assert pl.cdiv(nq * bq, bk) * bk <= t_pad
    return pl.pallas_call(
        functools.partial(_diff_prompt_kernel, bq=bq, bk=bk, lam_init=lam_init),
        out_shape=jax.ShapeDtypeStruct((t_pad, n_h * HEAD_W), F32),
        grid=(n_h, nq),
        in_specs=[pl.BlockSpec((4, DIFF_DH), lambda h, i: (0, 0)),
                  pl.BlockSpec((1, HEAD_W), lambda h, i: (0, 0)),
                  pl.BlockSpec((1, bq, HEAD_W), lambda h, i: (h, i, 0)),
                  pl.BlockSpec((1, t_pad, HEAD_W), lambda h, i: (h, 0, 0)),
                  pl.BlockSpec((1, t_pad, HEAD_W), lambda h, i: (h, 0, 0))],
        out_specs=pl.BlockSpec((bq, HEAD_W), lambda h, i: (i, h)),
        scratch_shapes=[pltpu.VMEM((2 * bq, 1), F32), pltpu.VMEM((2 * bq, 1), F32),
                        pltpu.VMEM((2 * bq, HEAD_W), F32)],
        compiler_params=_cparams(("arbitrary", "arbitrary")),
        name="diff_prompt",
    )(lqk, dn, q, k, v)


def _bf16_round(x):
    return x.astype(BF16).astype(F32)


def _diff_sample_kernel(pt_ref, lqk_ref, dn_ref, q_ref, kn_ref, vn_ref, *refs, n_tok, pages_per_step, lam_init):
    g_pages = pages_per_step
    k_refs = refs[:g_pages]
    v_refs = refs[g_pages:2 * g_pages]
    o_ref = refs[2 * g_pages]
    qq_ref, m_ref, l_ref, acc_ref = refs[2 * g_pages + 1:]
    n_h = q_ref.shape[1]
    n2 = 2 * n_tok
    p = pl.program_id(1)

    @pl.when(p == 0)
    def _():
        qrow = lax.broadcasted_iota(jnp.int32, (n2, 1), 0) % n_tok
        for h in range(n_h):
            qq = _split_maps(q_ref[0, h]) * (DIFF_DH ** -0.5)
            qq_ref[h] = qq.astype(F32)
            qf = qq.astype(F32)
            kn = _bf16_round(kn_ref[0, h])
            vn = _bf16_round(vn_ref[0, h])
            s = [jnp.where(qrow >= t, jnp.sum(qf * kn[t:t + 1], axis=-1, keepdims=True), NEG) for t in range(n_tok)]
            m = s[0]
            for t in range(1, n_tok):
                m = jnp.maximum(m, s[t])
            l = jnp.zeros((n2, 1), F32)
            acc = jnp.zeros((n2, HEAD_W), F32)
            for t in range(n_tok):
                pt = jnp.exp(s[t] - m)
                l = l + pt
                acc = acc + _bf16_round(pt) * vn[t:t + 1]
            m_ref[h] = jnp.broadcast_to(m, (n2, HEAD_W))
            l_ref[h] = jnp.broadcast_to(l, (n2, HEAD_W))
            acc_ref[h] = acc

    for g in range(g_pages):
        for h in range(n_h):
            k = k_refs[g][0, 0, h].astype(BF16)
            v = v_refs[g][0, 0, h].astype(BF16)
            s = _nt_dot(qq_ref[h].astype(BF16), k)
            m_old = m_ref[h]
            m_new = jnp.maximum(m_old, jnp.max(s, axis=-1, keepdims=True))
            alpha = jnp.exp(m_old - m_new)
            pr = jnp.exp(s - m_new)
            l_ref[h] = alpha * l_ref[h] + jnp.sum(pr, axis=-1, keepdims=True)
            acc_ref[h] = alpha * acc_ref[h] + _dot(pr.astype(BF16), v)
            m_ref[h] = m_new

    @pl.when(p == pl.num_programs(1) - 1)
    def _():
        lam = _diff_lambda(lqk_ref, lam_init)
        outs = []
        for h in range(n_h):
            on = acc_ref[h] / l_ref[h]
            o = on[:n_tok] - lam * on[n_tok:]
            outs.append(_rms(o, dn_ref[...]) * (1.0 - lam_init))
        o_ref[0] = jnp.concatenate(outs, axis=1)


def _page_specs(layer, n_pages, pages_per_step, n_h, page, reverse):
    specs = []
    for g in range(pages_per_step):
        def imap(b, p, pt, g=g):
            idx = p * pages_per_step + g
            if reverse:
                idx = n_pages - 1 - idx
            return (layer, pt[b * n_pages + idx], 0, 0, 0)
        specs.append(pl.BlockSpec((1, 1, n_h, page, HEAD_W), imap))
    return specs


def _diff_sample(q, kn, vn, cache_k, cache_v, page_table, layer, lqk, dn, *, lam_init, pages_per_step=4):
    n_seq, n_h, n_tok, _ = q.shape
    n_pages = page_table.shape[1]
    page = cache_k.shape[3]
    assert n_pages % pages_per_step == 0
    tok_spec = pl.BlockSpec((1, n_h, n_tok, HEAD_W), lambda b, p, pt: (b, 0, 0, 0))
    pspecs = _page_specs(layer, n_pages, pages_per_step, n_h, page, reverse=False)
    out = pl.pallas_call(
        functools.partial(_diff_sample_kernel, n_tok=n_tok, pages_per_step=pages_per_step, lam_init=lam_init),
        out_shape=jax.ShapeDtypeStruct((n_seq, n_tok, n_h * HEAD_W), F32),
        grid_spec=pltpu.PrefetchScalarGridSpec(
            num_scalar_prefetch=1,
            grid=(n_seq, n_pages // pages_per_step),
            in_specs=[pl.BlockSpec((4, DIFF_DH), lambda b, p, pt: (0, 0)),
                      pl.BlockSpec((1, HEAD_W), lambda b, p, pt: (0, 0)),
                      tok_spec, tok_spec, tok_spec] + pspecs + pspecs,
            out_specs=pl.BlockSpec((1, n_tok, n_h * HEAD_W), lambda b, p, pt: (b, 0, 0)),
            scratch_shapes=[pltpu.VMEM((n_h, 2 * n_tok, HEAD_W), F32)] * 4,
        ),
        compiler_params=_cparams(("arbitrary", "arbitrary")),
        name="diff_sample",
    )(page_table.reshape(-1), lqk, dn, q, kn, vn, *([cache_k] * pages_per_step), *([cache_v] * pages_per_step))
    return out.reshape(n_seq * n_tok, n_h * HEAD_W)


def _later_keys_matrix(n):
    ri = lax.broadcasted_iota(jnp.int32, (n, n), 0)
    ci = lax.broadcasted_iota(jnp.int32, (n, n), 1)
    return jnp.where(ri > ci, 1.0, 0.0).astype(BF16)


def _tail_sums(lk, u):
    n = lk.shape[0]
    hi = lk.astype(BF16)
    lo = (lk - hi.astype(F32)).astype(BF16)
    t = _dot(jnp.concatenate([hi, lo], axis=0), u)
    return t[:n] + t[n:]


def _sb_prompt_kernel(q_ref, k_ref, v_ref, o_ref, c_ref, acc_ref, *, blk):
    i = pl.program_id(1)
    q = q_ref[0]
    u = _later_keys_matrix(blk)
    row = i * blk + lax.broadcasted_iota(jnp.int32, (blk, 1), 0)
    c_ref[...] = jnp.zeros_like(c_ref)
    acc_ref[...] = jnp.zeros_like(acc_ref)

    def chunk(j, masked):
        off = pl.multiple_of(j * blk, blk)
        k = k_ref[0, pl.ds(off, blk), :]
        v = v_ref[0, pl.ds(off, blk), :]
        z = _nt_dot(q, k) * (SB_DH ** -0.5)
        sp = _softplus(z)
        lk = -sp
        lb = z - sp
        if masked:
            col = j * blk + lax.broadcasted_iota(jnp.int32, (1, blk), 1)
            vis = col < row
            lk = jnp.where(vis, lk, 0.0)
        c = c_ref[...]
        a = jnp.exp(lb + _tail_sums(lk, u) + c)
        if masked:
            a = jnp.where(vis, a, 0.0)
        acc_ref[...] += _dot(a.astype(BF16), v)
        c_ref[...] = c + jnp.sum(lk, axis=-1, keepdims=True)

    chunk(i, True)

    def body(t, carry):
        chunk(i - 1 - t, False)
        return carry

    lax.fori_loop(0, i, body, 0)
    o_ref[...] = acc_ref[...]


def _sb_prompt(q, k, v, *, t_valid, blk=256):
    n_h, t_pad, _ = q.shape
    nq = pl.cdiv(t_valid, blk)
    assert nq * blk <= t_pad
    return pl.pallas_call(
        functools.partial(_sb_prompt_kernel, blk=blk),
        out_shape=jax.ShapeDtypeStruct((t_pad, n_h * HEAD_W), F32),
        grid=(n_h, nq),
        in_specs=[pl.BlockSpec((1, blk, HEAD_W), lambda h, i: (h, i, 0)),
                  pl.BlockSpec((1, t_pad, HEAD_W), lambda h, i: (h, 0, 0)),
                  pl.BlockSpec((1, t_pad, HEAD_W), lambda h, i: (h, 0, 0))],
        out_specs=pl.BlockSpec((blk, HEAD_W), lambda h, i: (i, h)),
        scratch_shapes=[pltpu.VMEM((blk, 1), F32), pltpu.VMEM((blk, HEAD_W), F32)],
        compiler_params=_cparams(("arbitrary", "arbitrary")),
        name="sb_prompt",
    )(q, k, v)


def _sb_sample_kernel(pt_ref, q_ref, kn_ref, vn_ref, *refs, n_tok, pages_per_step):
    g_pages = pages_per_step
    k_refs = refs[:g_pages]
    v_refs = refs[g_pages:2 * g_pages]
    o_ref = refs[2 * g_pages]
    q8_ref, c_ref, acc_ref = refs[2 * g_pages + 1:]
    n_h = q_ref.shape[1]
    page = k_refs[0].shape[3]
    rows8 = 8
    p = pl.program_id(1)
    u = _later_keys_matrix(page)

    @pl.when(p == 0)
    def _():
        qrow = lax.broadcasted_iota(jnp.int32, (rows8, 1), 0)
        for h in range(n_h):
            q8 = jnp.concatenate([q_ref[0, h].astype(F32), jnp.zeros((rows8 - n_tok, HEAD_W), F32)], axis=0)
            q8_ref[h] = q8
            kn = _bf16_round(kn_ref[0, h])
            vn = _bf16_round(vn_ref[0, h])
            c = jnp.zeros((rows8, 1), F32)
            acc = jnp.zeros((rows8, HEAD_W), F32)
            for t in range(n_tok - 1, -1, -1):
                z = jnp.sum(q8 * kn[t:t + 1], axis=-1, keepdims=True) * (SB_DH ** -0.5)
                sp = _softplus(z)
                vis = qrow > t
                a = jnp.where(vis, jnp.exp(z - sp + c), 0.0)
                acc = acc + _bf16_round(a) * vn[t:t + 1]
                c = c + jnp.where(vis, -sp, 0.0)
            c_ref[h] = jnp.broadcast_to(c, (rows8, HEAD_W))
            acc_ref[h] = acc

    for g in range(g_pages):
        for h in range(n_h):
            k = k_refs[g][0, 0, h].astype(BF16)
            v = v_refs[g][0, 0, h].astype(BF16)
            z = _nt_dot(q8_ref[h].astype(BF16), k) * (SB_DH ** -0.5)
            sp = _softplus(z)
            lk = -sp
            a = jnp.exp(z - sp + _tail_sums(lk, u) + c_ref[h])
            acc_ref[h] += _dot(a.astype(BF16), v)
            c_ref[h] += jnp.sum(lk, axis=-1, keepdims=True)

    @pl.when(p == pl.num_programs(1) - 1)
    def _():
        o_ref[0] = jnp.concatenate([acc_ref[h][:n_tok] for h in range(n_h)], axis=1)


def _sb_sample(q, kn, vn, cache_k, cache_v, page_table, layer, *, pages_per_step=4):
    n_seq, n_h, n_tok, _ = q.shape
    n_pages = page_table.shape[1]
    page = cache_k.shape[3]
    assert n_pages % pages_per_step == 0 and n_tok <= 8
    tok_spec = pl.BlockSpec((1, n_h, n_tok, HEAD_W), lambda b, p, pt: (b, 0, 0, 0))
    pspecs = _page_specs(layer, n_pages, pages_per_step, n_h, page, reverse=True)
    out = pl.pallas_call(
        functools.partial(_sb_sample_kernel, n_tok=n_tok, pages_per_step=pages_per_step),
        out_shape=jax.ShapeDtypeStruct((n_seq, n_tok, n_h * HEAD_W), F32),
        grid_spec=pltpu.PrefetchScalarGridSpec(
            num_scalar_prefetch=1,
            grid=(n_seq, n_pages // pages_per_step),
            in_specs=[tok_spec, tok_spec, tok_spec] + pspecs + pspecs,
            out_specs=pl.BlockSpec((1, n_tok, n_h * HEAD_W), lambda b, p, pt: (b, 0, 0)),
            scratch_shapes=[pltpu.VMEM((n_h, 8, HEAD_W), F32)] * 3,
        ),
        compiler_params=_cparams(("arbitrary", "arbitrary")),
        name="sb_sample",
    )(page_table.reshape(-1), q, kn, vn, *([cache_k] * pages_per_step), *([cache_v] * pages_per_step))
    return out.reshape(n_seq * n_tok, n_h * HEAD_W)


def _merge_kernel(x_ref, og_ref, gg_ref, od_ref, dg_ref, os_ref, sg_ref, w_ref, o_ref):
    ag = (og_ref[...] * _silu(gg_ref[...])).astype(BF16)
    ad = (od_ref[...] * _silu(dg_ref[...])).astype(BF16)
    asb = (os_ref[...] * _silu(sg_ref[...])).astype(BF16)
    y = _dot(ag, w_ref[0:GLA_WIDTH, :])
    y += _dot(ad, w_ref[GLA_WIDTH:GLA_WIDTH + ATT_WIDTH, :])
    y += _dot(asb, w_ref[GLA_WIDTH + ATT_WIDTH:, :])
    o_ref[...] = x_ref[...] + y


def _merge(x, og, pg, od, dg, osb, sg, w_out, *, tm):
    t_valid, d = x.shape
    t_pad = og.shape[0]
    assert PG_G % GLA_WIDTH == 0
    return pl.pallas_call(
        _merge_kernel,
        out_shape=jax.ShapeDtypeStruct((t_valid, d), F32),
        grid=(t_pad // tm,),
        in_specs=[pl.BlockSpec((tm, d), lambda i: (i, 0)),
                  pl.BlockSpec((tm, GLA_WIDTH), lambda i: (i, 0)),
                  pl.BlockSpec((tm, GLA_WIDTH), lambda i: (i, PG_G // GLA_WIDTH)),
                  pl.BlockSpec((tm, ATT_WIDTH), lambda i: (i, 0)),
                  pl.BlockSpec((tm, ATT_WIDTH), lambda i: (i, 0)),
                  pl.BlockSpec((tm, ATT_WIDTH), lambda i: (i, 0)),
                  pl.BlockSpec((tm, ATT_WIDTH), lambda i: (i, 0)),
                  pl.BlockSpec(w_out.shape, lambda i: (0, 0))],
        out_specs=pl.BlockSpec((tm, d), lambda i: (i, 0)),
        compiler_params=_cparams(("arbitrary",)),
        name="merge",
    )(x, og, pg, od, dg, osb, sg, w_out)


def _final_norm_kernel(x_ref, g_ref, o_ref):
    o_ref[...] = _rms(x_ref[...], g_ref[...])


def _final_norm(x, g, *, tm):
    t, d = x.shape
    return pl.pallas_call(
        _final_norm_kernel,
        out_shape=jax.ShapeDtypeStruct((t, d), F32),
        grid=(pl.cdiv(t, tm),),
        in_specs=[pl.BlockSpec((tm, d), lambda i: (i, 0)), pl.BlockSpec((1, d), lambda i: (0, 0))],
        out_specs=pl.BlockSpec((tm, d), lambda i: (i, 0)),
        compiler_params=_cparams(("arbitrary",)),
        name="final_norm",
    )(x, g.reshape(1, d))


def _head_major_to_cache(a):
    return jnp.transpose(a, (1, 0, 2))


def kernel(x_prompt, x_sample, cache_diff_k, cache_diff_v, cache_sb_k, cache_sb_v, state_gla, page_table, meta_tokens, norm_mix, w_in, gla_w_gate, gla_b_gate, gla_norm, diff_lambda_q1, diff_lambda_k1, diff_lambda_q2, diff_lambda_k2, diff_norm, w_out, final_norm):
    n_batch, seq, d = x_prompt.shape
    assert n_batch == 1
    n_seq, n_tok, _ = x_sample.shape
    depth = w_in.shape[0]
    t_valid = seq + META_LEN
    row_tile = 512
    t_pad = pl.cdiv(t_valid, row_tile) * row_tile
    n_s = n_seq * n_tok

    xp = jnp.concatenate([meta_tokens.astype(x_prompt.dtype), x_prompt[0]], axis=0)
    xs = x_sample.reshape(n_s, d)
    caches = [jnp.transpose(c, (0, 1, 3, 2, 4)) for c in (cache_diff_k, cache_diff_v, cache_sb_k, cache_sb_v)]

    p_rows, s_rows = [], []
    for l in range(depth):
        lam_init = 0.8 - 0.6 * math.exp(-0.3 * l)
        w_packed = _pack_w_in(w_in[l])
        w_out_bf = w_out[l].astype(BF16)
        wg_pad = jnp.zeros((COL_TILE, COL_TILE), F32).at[:GLA_RANK].set(gla_w_gate[l]).astype(BF16)
        bg = gla_b_gate[l].reshape(1, -1)
        gn = gla_norm[l].reshape(1, -1)
        dn = diff_norm[l].reshape(1, -1)
        lqk = jnp.stack([diff_lambda_q1[l], diff_lambda_k1[l], diff_lambda_q2[l], diff_lambda_k2[l]])

        pr = _inproj(xp, norm_mix[l], w_packed, t_pad=t_pad, tm=row_tile)
        og, s_fin = _gla_prompt(pr["pg"], wg_pad, bg, gn, t_valid=t_valid)
        od = _diff_prompt(pr["dq"], pr["dkb"], pr["dvb"], lqk, dn, t_valid=t_valid, lam_init=lam_init)
        osb = _sb_prompt(pr["sq"], pr["skb"], pr["svb"], t_valid=t_valid)
        xp = _merge(xp, og, pr["pg"], od, pr["dg"], osb, pr["sg"], w_out_bf, tm=row_tile)
        p_rows.append(tuple(_head_major_to_cache(pr[k])[None] for k in ("dk", "dv", "sk", "sv")) + (s_fin[None],))

        sr = _inproj(xs, norm_mix[l], w_packed, t_pad=n_s, tm=n_s)
        og_s, s_new = _gla_sample(sr["pg"], state_gla, l, wg_pad, bg, gn, n_seq=n_seq, n_tok=n_tok)
        per_seq = lambda a: a.reshape(a.shape[0], n_seq, n_tok, HEAD_W).transpose(1, 0, 2, 3)
        od_s = _diff_sample(per_seq(sr["dq"]), per_seq(sr["dk"]), per_seq(sr["dv"]), caches[0], caches[1],
                            page_table, l, lqk, dn, lam_init=lam_init)
        os_s = _sb_sample(per_seq(sr["sq"]), per_seq(sr["sk"]), per_seq(sr["sv"]), caches[2], caches[3],
                          page_table, l)
        xs = _merge(xs, og_s, sr["pg"], od_s, sr["dg"], os_s, sr["sg"], w_out_bf, tm=n_s)
        to_cache = lambda a: a.reshape(a.shape[0], n_seq, n_tok, HEAD_W).transpose(1, 2, 0, 3)
        s_rows.append(tuple(to_cache(sr[k]) for k in ("dk", "dv", "sk", "sv")) + (s_new,))

    y_prompt = _final_norm(xp[META_LEN:], final_norm, tm=row_tile)[None]
    y_sample = _final_norm(xs, final_norm, tm=n_s).reshape(n_seq, n_tok, d)
    stack = lambda rows, i: jnp.stack([r[i] for r in rows])
    return (y_prompt, y_sample,
            stack(p_rows, 0), stack(p_rows, 1), stack(p_rows, 2), stack(p_rows, 3), stack(p_rows, 4),
            stack(s_rows, 0), stack(s_rows, 1), stack(s_rows, 2), stack(s_rows, 3), stack(s_rows, 4))
```

```python
import functools
import math

import jax
import jax.numpy as jnp
from jax import lax
from jax.experimental import pallas as pl
from jax.experimental.pallas import tpu as pltpu

F32 = jnp.float32
BF16 = jnp.bfloat16

META_LEN = 16
GLA_HEADS = 4
GLA_DK = 64
GLA_DV = 128
GLA_RANK = 16
GLA_TAU = 16.0
DIFF_HEADS = 6
DIFF_DH = 64
SB_HEADS = 6
SB_DH = 128
HEAD_W = 128
EPS = 1e-6
NEG = -1e30
EXP_UNDERFLOW = -110.0

GLA_WIDTH = GLA_HEADS * GLA_DV
ATT_WIDTH = DIFF_HEADS * HEAD_W
IN_WIDTHS = (256, 256, 512, 16, 512, 768, 768, 768, 768, 768, 768, 768, 768)

COL_TILE = 256
PG_WIDTH = 7 * COL_TILE
PG_Q, PG_K, PG_V, PG_G, PG_R = 0, 256, 512, 1024, 1536
SEGMENTS = (("pg", 7), ("dq", 3), ("dk", 3), ("dv", 3), ("dg", 3), ("sq", 3), ("sk", 3), ("sv", 3), ("sg", 3))
N_COL_TILES = sum(n for _, n in SEGMENTS)

VMEM_LIMIT = 56 * 1024 * 1024
DIFF_KEY_CHUNK = 512
SB_KEY_CHUNK = 256
PAGES_PER_STEP = 8


def _cparams(sem):
    return pltpu.CompilerParams(dimension_semantics=sem, vmem_limit_bytes=VMEM_LIMIT)


def _softplus(z):
    return jnp.maximum(z, 0.0) + jnp.log1p(jnp.exp(-jnp.abs(z)))


def _log_sigmoid(z):
    return -_softplus(-z)


def _silu(g):
    return g / (1.0 + jnp.exp(-g))


def _nt_dot(a, b):
    return lax.dot_general(a, b, (((1,), (1,)), ((), ())), preferred_element_type=F32)


def _dot(a, b):
    return jnp.dot(a, b, preferred_element_type=F32)


def _rms(o, g):
    return o * lax.rsqrt(jnp.mean(o * o, axis=-1, keepdims=True) + EPS) * g


def _pack_w_in(w):
    offs = [0]
    for wd in IN_WIDTHS:
        offs.append(offs[-1] + wd)
    c = [w[:, offs[i]:offs[i + 1]] for i in range(len(IN_WIDTHS))]
    gq, gk, gv, gr, gg = c[:5]
    pad = jnp.zeros((w.shape[0], PG_WIDTH - PG_R - GLA_RANK), w.dtype)
    return jnp.concatenate([gq, gk, gv, gg, gr, pad] + c[5:], axis=1).astype(BF16)


def _inproj_kernel(x_ref, g_ref, w_ref, *refs, t_valid, tm, outs):
    out_refs = refs[:len(outs)]
    h_ref = refs[len(outs)]
    i = pl.program_id(0)
    j = pl.program_id(1)

    @pl.when(j == 0)
    def _():
        x = x_ref[...]
        y = x * lax.rsqrt(jnp.mean(x * x, axis=-1, keepdims=True) + EPS) * g_ref[...]
        rows = i * tm + lax.broadcasted_iota(jnp.int32, (tm, 1), 0)
        h_ref[...] = jnp.where(rows < t_valid, y, 0.0).astype(BF16)

    segments = sorted({(start, n) for start, n, _ in outs})
    for start, n in segments:
        @pl.when((j >= start) & (j < start + n))
        def _(start=start, n=n):
            acc = _dot(h_ref[...], w_ref[...])
            for o_ref, (o_start, _, kind) in zip(out_refs, outs):
                if o_start != start:
                    continue
                if kind == "rows":
                    o_ref[...] = acc.astype(o_ref.dtype)
                    continue
                for hh in range(2):
                    a = acc[:, HEAD_W * hh:HEAD_W * (hh + 1)]
                    if kind == "heads":
                        o_ref[hh] = a.astype(o_ref.dtype)
                    else:
                        at = a.T
                        tw = o_ref.shape[-1]
                        for c in range(tm // tw):
                            o_ref[hh, c] = at[:, c * tw:(c + 1) * tw].astype(o_ref.dtype)


def _inproj(x, g, w_packed, *, t_pad, tm, attention_copies):
    t_valid, d = x.shape
    assert t_pad % tm == 0
    starts = {}
    s = 0
    for name, n in SEGMENTS:
        starts[name] = (s, n)
        s += n
    spec = [("pg", "pg", "rows", F32, t_pad, 0)]
    for a in ("d", "s"):
        spec += [(a + "q", a + "q", "heads", BF16, t_pad, 0),
                 (a + "k", a + "k", "heads", F32, t_valid, 0),
                 (a + "v", a + "v", "heads", F32, t_valid, 0),
                 (a + "g", a + "g", "rows", F32, t_pad, 0)]
        if attention_copies is not None:
            assert tm % attention_copies[a] == 0
            spec += [(a + "kb", a + "k", "heads", BF16, t_pad, 0),
                     (a + "vt", a + "v", "heads_t", BF16, t_pad, attention_copies[a])]
    out_shapes, out_specs, outs = [], [], []
    for key, seg, kind, dt, rows, tw in spec:
        start, n = starts[seg]
        outs.append((start, n, kind))
        if kind == "heads":
            out_shapes.append(jax.ShapeDtypeStruct((2 * n, rows, HEAD_W), dt))
            out_specs.append(pl.BlockSpec(
                (2, tm, HEAD_W), lambda i, j, start=start, n=n: (jnp.clip(j - start, 0, n - 1), i, 0)))
        elif kind == "heads_t":
            out_shapes.append(jax.ShapeDtypeStruct((2 * n, rows // tw, HEAD_W, tw), dt))
            out_specs.append(pl.BlockSpec(
                (2, tm // tw, HEAD_W, tw), lambda i, j, start=start, n=n: (jnp.clip(j - start, 0, n - 1), i, 0, 0)))
        else:
            out_shapes.append(jax.ShapeDtypeStruct((rows, n * COL_TILE), dt))
            out_specs.append(pl.BlockSpec(
                (tm, COL_TILE), lambda i, j, start=start, n=n: (i, jnp.clip(j - start, 0, n - 1))))
    res = pl.pallas_call(
        functools.partial(_inproj_kernel, t_valid=t_valid, tm=tm, outs=tuple(outs)),
        out_shape=out_shapes,
        grid=(t_pad // tm, N_COL_TILES),
        in_specs=[pl.BlockSpec((tm, d), lambda i, j: (i, 0)),
                  pl.BlockSpec((1, d), lambda i, j: (0, 0)),
                  pl.BlockSpec((d, COL_TILE), lambda i, j: (0, j))],
        out_specs=out_specs,
        scratch_shapes=[pltpu.VMEM((tm, d), BF16)],
        compiler_params=_cparams(("arbitrary", "arbitrary")),
        name="inproj",
    )(x, g.reshape(1, d), w_packed)
    return {key: r for (key, *_), r in zip(spec, res)}


def _cumsum_rows(lg, n):
    ri = lax.broadcasted_iota(jnp.int32, (n, n), 0)
    ci = lax.broadcasted_iota(jnp.int32, (n, n), 1)
    tri = jnp.where(ri >= ci, 1.0, 0.0).astype(BF16)
    hi = lg.astype(BF16)
    lo = (lg - hi.astype(F32)).astype(BF16)
    return _dot(tri, hi) + _dot(tri, lo)


def _gla_prompt_kernel(pg_ref, wg_ref, bg_ref, gn_ref, og_ref, sfin_ref, s_ref, *, t_valid, chunk, sub):
    step = pl.program_id(0)

    @pl.when(step == 0)
    def _():
        s_ref[...] = jnp.zeros_like(s_ref)

    rows = step * chunk + lax.broadcasted_iota(jnp.int32, (chunk, 1), 0)
    x = _dot(pg_ref[:, PG_R:PG_R + COL_TILE].astype(BF16), wg_ref[...]) + bg_ref[...]
    lg = jnp.where(rows < t_valid, _log_sigmoid(x) * (1.0 / GLA_TAU), 0.0)
    b = _cumsum_rows(lg, chunk)

    lane = lax.broadcasted_iota(jnp.int32, (1, HEAD_W), 1)
    head_mask = (lane < GLA_DK, lane >= GLA_DK)
    n_sub = chunk // sub
    ri = lax.broadcasted_iota(jnp.int32, (sub, chunk), 0)
    ci = lax.broadcasted_iota(jnp.int32, (sub, chunk), 1)
    krow = lax.broadcasted_iota(jnp.int32, (chunk, 1), 0)

    for p in range(2):
        qp = pg_ref[:, PG_Q + HEAD_W * p:PG_Q + HEAD_W * (p + 1)] * (GLA_DK ** -0.5)
        kp = pg_ref[:, PG_K + HEAD_W * p:PG_K + HEAD_W * (p + 1)]
        bp = b[:, HEAD_W * p:HEAD_W * (p + 1)]
        s_old = s_ref[p]
        s_bf = s_old.astype(BF16)
        qe = qp * jnp.exp(bp)
        b_end = bp[chunk - 1:chunk]
        kend_t = (kp * jnp.exp(b_end - bp)).T
        dec_t = jnp.broadcast_to(jnp.exp(b_end), (HEAD_W, HEAD_W)).T
        q_blk, k_blk = [], []
        for blk in range(n_sub):
            lo, hi = blk * sub, (blk + 1) * sub
            r = jnp.zeros((1, HEAD_W), F32) if blk == 0 else bp[lo - 1:lo]
            q_blk.append(qp[lo:hi] * jnp.exp(bp[lo:hi] - r))
            k_blk.append((kp * jnp.exp(jnp.where(krow < hi, r - bp, 0.0))).astype(BF16))
        upd = []
        for hh in range(2):
            h = 2 * p + hh
            vb = pg_ref[:, PG_V + GLA_DV * h:PG_V + GLA_DV * (h + 1)].astype(BF16)
            o_inter = _dot(jnp.where(head_mask[hh], qe, 0.0).astype(BF16), s_bf)
            o_rows = []
            for blk in range(n_sub):
                lo, hi = blk * sub, (blk + 1) * sub
                att = _nt_dot(jnp.where(head_mask[hh], q_blk[blk], 0.0).astype(BF16), k_blk[blk])
                att = jnp.where(ci <= ri + lo, att, 0.0)
                o_rows.append(_dot(att.astype(BF16), vb))
            o = o_inter + jnp.concatenate(o_rows, axis=0)
            og_ref[:, GLA_DV * h:GLA_DV * (h + 1)] = _rms(o, gn_ref[...])
            upd.append(_dot(kend_t[GLA_DK * hh:GLA_DK * (hh + 1)].astype(BF16), vb))
        s_ref[p] = dec_t * s_old + jnp.concatenate(upd, axis=0)

    @pl.when(step == pl.num_programs(0) - 1)
    def _():
        sfin_ref[...] = s_ref[...]


def _gla_prompt(pg, wg_pad, bg, gn, *, t_valid, chunk=128, sub=16):
    t_pad = pg.shape[0]
    n_steps = pl.cdiv(t_valid, chunk)
    assert n_steps * chunk <= t_pad
    og, sfin = pl.pallas_call(
        functools.partial(_gla_prompt_kernel, t_valid=t_valid, chunk=chunk, sub=sub),
        out_shape=[jax.ShapeDtypeStruct((t_pad, GLA_WIDTH), F32),
                   jax.ShapeDtypeStruct((2, HEAD_W, GLA_DV), F32)],
        grid=(n_steps,),
        in_specs=[pl.BlockSpec((chunk, PG_WIDTH), lambda i: (i, 0)),
                  pl.BlockSpec((COL_TILE, COL_TILE), lambda i: (0, 0)),
                  pl.BlockSpec((1, COL_TILE), lambda i: (0, 0)),
                  pl.BlockSpec((1, GLA_DV), lambda i: (0, 0))],
        out_specs=[pl.BlockSpec((chunk, GLA_WIDTH), lambda i: (i, 0)),
                   pl.BlockSpec((2, HEAD_W, GLA_DV), lambda i: (0, 0, 0))],
        scratch_shapes=[pltpu.VMEM((2, HEAD_W, GLA_DV), F32)],
        compiler_params=_cparams(("arbitrary",)),
        name="gla_prompt",
    )(pg, wg_pad, bg, gn)
    return og, sfin.reshape(GLA_HEADS, GLA_DK, GLA_DV)


def _gla_sample_kernel(pg_ref, st_ref, wg_ref, bg_ref, gn_ref, og_ref, snew_ref, *, n_tok):
    pad = HEAD_W - n_tok
    zpad = jnp.zeros((pad, HEAD_W), F32)
    gr = jnp.concatenate([pg_ref[0, :,PG_R:PG_R + COL_TILE], jnp.zeros((pad, COL_TILE), F32)], axis=0)
    x = _dot(gr.astype(BF16), wg_ref[...]) + bg_ref[...]
    rows = lax.broadcasted_iota(jnp.int32, (HEAD_W, 1), 0)
    lg = jnp.where(rows < n_tok, _log_sigmoid(x) * (1.0 / GLA_TAU), 0.0)
    b = _cumsum_rows(lg, HEAD_W)

    lane = lax.broadcasted_iota(jnp.int32, (1, HEAD_W), 1)
    head_mask = (lane < GLA_DK, lane >= GLA_DK)
    ri = lax.broadcasted_iota(jnp.int32, (HEAD_W, HEAD_W), 0)
    ci = lax.broadcasted_iota(jnp.int32, (HEAD_W, HEAD_W), 1)
    og = []
    for p in range(2):
        qp = jnp.concatenate([pg_ref[0, :,PG_Q + HEAD_W * p:PG_Q + HEAD_W * (p + 1)], zpad], axis=0) * (GLA_DK ** -0.5)
        kp = jnp.concatenate([pg_ref[0, :,PG_K + HEAD_W * p:PG_K + HEAD_W * (p + 1)], zpad], axis=0)
        bp = b[:, HEAD_W * p:HEAD_W * (p + 1)]
        s_old = jnp.concatenate([st_ref[0, 0, 2 * p], st_ref[0, 0, 2 * p + 1]], axis=0)
        s_bf = s_old.astype(BF16)
        qe = qp * jnp.exp(bp)
        kinv = (kp * jnp.exp(-bp)).astype(BF16)
        b_end = bp[HEAD_W - 1:HEAD_W]
        kend_t = (kp * jnp.exp(b_end - bp)).T
        dec_t = jnp.broadcast_to(jnp.exp(b_end), (HEAD_W, HEAD_W)).T
        upd = []
        for hh in range(2):
            h = 2 * p + hh
            vb = jnp.concatenate([pg_ref[0, :,PG_V + GLA_DV * h:PG_V + GLA_DV * (h + 1)], zpad], axis=0).astype(BF16)
            qm = jnp.where(head_mask[hh], qe, 0.0).astype(BF16)
            att = jnp.where(ci <= ri, _nt_dot(qm, kinv), 0.0)
            o = _dot(qm, s_bf) + _dot(att.astype(BF16), vb)
            og.append(_rms(o, gn_ref[...])[:n_tok])
            upd.append(_dot(kend_t[GLA_DK * hh:GLA_DK * (hh + 1)].astype(BF16), vb))
        s_new = dec_t * s_old + jnp.concatenate(upd, axis=0)
        snew_ref[0, 2 * p] = s_new[:GLA_DK]
        snew_ref[0, 2 * p + 1] = s_new[GLA_DK:]
    og_ref[0] = jnp.concatenate(og, axis=1)


def _gla_sample(pg, state, layer, wg_pad, bg, gn, *, n_seq, n_tok):
    pg3 = pg.reshape(n_seq, n_tok, PG_WIDTH)
    og, snew = pl.pallas_call(
        functools.partial(_gla_sample_kernel, n_tok=n_tok),
        out_shape=[jax.ShapeDtypeStruct((n_seq, n_tok, GLA_WIDTH), F32),
                   jax.ShapeDtypeStruct((n_seq, GLA_HEADS, GLA_DK, GLA_DV), F32)],
        grid=(n_seq,),
        in_specs=[pl.BlockSpec((1, n_tok, PG_WIDTH), lambda i: (i, 0, 0)),
                  pl.BlockSpec((1, 1, GLA_HEADS, GLA_DK, GLA_DV), lambda i: (layer, i, 0, 0, 0)),
                  pl.BlockSpec((COL_TILE, COL_TILE), lambda i: (0, 0)),
                  pl.BlockSpec((1, COL_TILE), lambda i: (0, 0)),
                  pl.BlockSpec((1, GLA_DV), lambda i: (0, 0))],
        out_specs=[pl.BlockSpec((1, n_tok, GLA_WIDTH), lambda i: (i, 0, 0)),
                   pl.BlockSpec((1, GLA_HEADS, GLA_DK, GLA_DV), lambda i: (i, 0, 0, 0))],
        compiler_params=_cparams(("arbitrary",)),
        name="gla_sample",
    )(pg3, state, wg_pad, bg, gn)
    return og.reshape(n_seq * n_tok, GLA_WIDTH), snew


def _diff_lambda(lqk_ref, lam_init):
    s1 = jnp.sum(lqk_ref[0:1, :] * lqk_ref[1:2, :], axis=-1, keepdims=True)
    s2 = jnp.sum(lqk_ref[2:3, :] * lqk_ref[3:4, :], axis=-1, keepdims=True)
    return jnp.exp(s1) - jnp.exp(s2) + lam_init


def _split_maps(q):
    lane = lax.broadcasted_iota(jnp.int32, q.shape, 1)
    zero = jnp.zeros_like(q)
    return jnp.concatenate([jnp.where(lane < DIFF_DH, q, zero), jnp.where(lane >= DIFF_DH, q, zero)], axis=0)


def _diff_prompt_kernel(lqk_ref, dn_ref, q_ref, k_ref, vt_ref, o_ref, m_ref, l_ref, acc_ref, s_ref, *, bq, bk,
                        lam_init):
    i = pl.program_id(1)
    qq = _split_maps(q_ref[0]) * (DIFF_DH ** -0.5)
    lane = lax.broadcasted_iota(jnp.int32, (1, 2 * bq), 1)
    qpos = i * bq + jnp.where(lane < bq, lane, lane - bq)
    m_ref[...] = jnp.full_like(m_ref, NEG)
    l_ref[...] = jnp.zeros_like(l_ref)
    acc_ref[...] = jnp.zeros_like(acc_ref)

    n_full = (i * bq + 1) // bk
    n_need = (i * bq + bq + bk - 1) // bk

    def scores(j):
        off = pl.multiple_of(j * bk, bk)
        return _nt_dot(k_ref[0, pl.ds(off, bk), :], qq)

    s_ref[...] = scores(0)

    def chunk(j, masked):
        s = s_ref[...]
        s_next = scores(jnp.minimum(j + 1, n_need - 1))
        if masked:
            kpos = j * bk + lax.broadcasted_iota(jnp.int32, (bk, 1), 0)
            s = jnp.where(kpos <= qpos, s, NEG)
        m_old = m_ref[...]
        m_new = jnp.maximum(m_old, jnp.max(s, axis=0, keepdims=True))
        alpha = jnp.exp(m_old - m_new)
        p = jnp.exp(s - m_new)
        l_ref[...] = alpha * l_ref[...] + jnp.sum(p, axis=0, keepdims=True)
        acc_ref[...] = alpha * acc_ref[...] + _dot(vt_ref[0, j], p.astype(BF16))
        m_ref[...] = m_new
        s_ref[...] = s_next

    def full_body(j, c):
        chunk(j, False)
        return c

    def diag_body(j, c):
        chunk(j, True)
        return c

    lax.fori_loop(0, n_full, full_body, 0)
    lax.fori_loop(n_full, n_need, diag_body, 0)

    lam = _diff_lambda(lqk_ref, lam_init)
    on = acc_ref[...] / l_ref[...]
    o = (on[:, :bq] - lam * on[:, bq:]).T
    o_ref[...] = _rms(o, dn_ref[...]) * (1.0 - lam_init)


def _diff_prompt(q, k, vt, lqk, dn, *, t_valid, lam_init, bq=256):
    n_h, t_pad, _ = q.shape
    bk = vt.shape[-1]
    nq = pl.cdiv(t_valid, bq)
    assert pl.cdiv(nq * bq, bk) * bk <= t_pad
    return pl.pallas_call(
        functools.partial(_diff_prompt_kernel, bq=bq, bk=bk, lam_init=lam_init),
        out_shape=jax.ShapeDtypeStruct((t_pad, n_h * HEAD_W), F32),
        grid=(n_h, nq),
        in_specs=[pl.BlockSpec((4, DIFF_DH), lambda h, i: (0, 0)),
                  pl.BlockSpec((1, HEAD_W), lambda h, i: (0, 0)),
                  pl.BlockSpec((1, bq, HEAD_W), lambda h, i: (h, i, 0)),
                  pl.BlockSpec((1, t_pad, HEAD_W), lambda h, i: (h, 0, 0)),
                  pl.BlockSpec((1, t_pad // bk, HEAD_W, bk), lambda h, i: (h, 0, 0, 0))],
        out_specs=pl.BlockSpec((bq, HEAD_W), lambda h, i: (i, h)),
        scratch_shapes=[pltpu.VMEM((1, 2 * bq), F32), pltpu.VMEM((1, 2 * bq), F32),
                        pltpu.VMEM((HEAD_W, 2 * bq), F32), pltpu.VMEM((bk, 2 * bq), F32)],
        compiler_params=_cparams(("arbitrary", "arbitrary")),
        name="diff_prompt",
    )(lqk, dn, q, k, vt)


def _bf16_round(x):
    return x.astype(BF16).astype(F32)


def _diff_sample_kernel(pt_ref, lqk_ref, dn_ref, q_ref, kn_ref, vn_ref, *refs, n_tok, pages_per_step, lam_init):
    g_pages = pages_per_step
    k_refs = refs[:g_pages]
    v_refs = refs[g_pages:2 * g_pages]
    o_ref = refs[2 * g_pages]
    qq_ref, m_ref, l_ref, acc_ref = refs[2 * g_pages + 1:]
    n_h = q_ref.shape[1]
    n2 = 2 * n_tok
    p = pl.program_id(1)

    @pl.when(p == 0)
    def _():
        qrow = lax.broadcasted_iota(jnp.int32, (n2, 1), 0) % n_tok
        for h in range(n_h):
            qq = _split_maps(q_ref[0, h]) * (DIFF_DH ** -0.5)
            qq_ref[h] = qq.astype(F32)
            qf = qq.astype(F32)
            kn = _bf16_round(kn_ref[0, h])
            vn = _bf16_round(vn_ref[0, h])
            s = [jnp.where(qrow >= t, jnp.sum(qf * kn[t:t + 1], axis=-1, keepdims=True), NEG) for t in range(n_tok)]
            m = s[0]
            for t in range(1, n_tok):
                m = jnp.maximum(m, s[t])
            l = jnp.zeros((n2, 1), F32)
            acc = jnp.zeros((n2, HEAD_W), F32)
            for t in range(n_tok):
                pt = jnp.exp(s[t] - m)
                l = l + pt
                acc = acc + _bf16_round(pt) * vn[t:t + 1]
            m_ref[h] = m
            l_ref[h] = l
            acc_ref[h] = acc

    page = k_refs[0].shape[3]
    scores = []
    for h in range(n_h):
        qh = qq_ref[h].astype(BF16)
        scores.append(jnp.concatenate(
            [_nt_dot(qh, k_refs[g][0, 0, h].astype(BF16)) for g in range(g_pages)], axis=1))
    probs, alphas = [], []
    for h in range(n_h):
        s = scores[h]
        m_old = m_ref[h]
        m_new = jnp.maximum(m_old, jnp.max(s, axis=-1, keepdims=True))
        alpha = jnp.exp(m_old - m_new)
        p32 = jnp.exp(s - m_new)
        l_ref[h] = alpha * l_ref[h] + jnp.sum(p32, axis=-1, keepdims=True)
        m_ref[h] = m_new
        probs.append(p32.astype(BF16))
        alphas.append(alpha)
    for h in range(n_h):
        pv = _dot(probs[h][:, :page], v_refs[0][0, 0, h].astype(BF16))
        for g in range(1, g_pages):
            pv += _dot(probs[h][:, g * page:(g + 1) * page], v_refs[g][0, 0, h].astype(BF16))
        acc_ref[h] = alphas[h] * acc_ref[h] + pv

    @pl.when(p == pl.num_programs(1) - 1)
    def _():
        lam = _diff_lambda(lqk_ref, lam_init)
        outs = []
        for h in range(n_h):
            on = acc_ref[h] / l_ref[h]
            o = on[:n_tok] - lam * on[n_tok:]
            outs.append(_rms(o, dn_ref[...]) * (1.0 - lam_init))
        o_ref[0] = jnp.concatenate(outs, axis=1)


def _page_specs(layer, n_pages, pages_per_step, n_h, page, reverse):
    specs = []
    for g in range(pages_per_step):
        def imap(b, p, pt, g=g):
            idx = p * pages_per_step + g
            if reverse:
                idx = n_pages - 1 - idx
            return (layer, pt[b * n_pages + idx], 0, 0, 0)
        specs.append(pl.BlockSpec((1, 1, n_h, page, HEAD_W), imap))
    return specs


def _diff_sample(q, kn, vn, cache_k, cache_v, page_table, layer, lqk, dn, *, lam_init, pages_per_step=PAGES_PER_STEP):
    n_seq, n_h, n_tok, _ = q.shape
    n_pages = page_table.shape[1]
    page = cache_k.shape[3]
    assert n_pages % pages_per_step == 0
    tok_spec = pl.BlockSpec((1, n_h, n_tok, HEAD_W), lambda b, p, pt: (b, 0, 0, 0))
    pspecs = _page_specs(layer, n_pages, pages_per_step, n_h, page, reverse=False)
    out = pl.pallas_call(
        functools.partial(_diff_sample_kernel, n_tok=n_tok, pages_per_step=pages_per_step, lam_init=lam_init),
        out_shape=jax.ShapeDtypeStruct((n_seq, n_tok, n_h * HEAD_W), F32),
        grid_spec=pltpu.PrefetchScalarGridSpec(
            num_scalar_prefetch=1,
            grid=(n_seq, n_pages // pages_per_step),
            in_specs=[pl.BlockSpec((4, DIFF_DH), lambda b, p, pt: (0, 0)),
                      pl.BlockSpec((1, HEAD_W), lambda b, p, pt: (0, 0)),
                      tok_spec, tok_spec, tok_spec] + pspecs + pspecs,
            out_specs=pl.BlockSpec((1, n_tok, n_h * HEAD_W), lambda b, p, pt: (b, 0, 0)),
            scratch_shapes=[pltpu.VMEM((n_h, 2 * n_tok, HEAD_W), F32), pltpu.VMEM((n_h, 2 * n_tok, 1), F32),
                            pltpu.VMEM((n_h, 2 * n_tok, 1), F32), pltpu.VMEM((n_h, 2 * n_tok, HEAD_W), F32)],
        ),
        compiler_params=_cparams(("arbitrary", "arbitrary")),
        name="diff_sample",
    )(page_table.reshape(-1), lqk, dn, q, kn, vn, *([cache_k] * pages_per_step), *([cache_v] * pages_per_step))
    return out.reshape(n_seq * n_tok, n_h * HEAD_W)


def _later_keys_matrix(n):
    ri = lax.broadcasted_iota(jnp.int32, (n, n), 0)
    ci = lax.broadcasted_iota(jnp.int32, (n, n), 1)
    return jnp.where(ri > ci, 1.0, 0.0).astype(BF16)


def _tail_sums(lk, u):
    n = lk.shape[0]
    hi = lk.astype(BF16)
    lo = (lk - hi.astype(F32)).astype(BF16)
    t = _dot(jnp.concatenate([hi, lo], axis=0), u)
    return t[:n] + t[n:]


def _sb_prompt_kernel(q_ref, k_ref, vt_ref, o_ref, c_ref, acc_ref, *, blk):
    i = pl.program_id(1)
    q = q_ref[0]
    ri = lax.broadcasted_iota(jnp.int32, (blk, blk), 0)
    ci = lax.broadcasted_iota(jnp.int32, (blk, blk), 1)
    later = jnp.where(ci > ri, 1.0, 0.0).astype(BF16)
    qpos = i * blk + lax.broadcasted_iota(jnp.int32, (1, blk), 1)
    c_ref[...] = jnp.zeros_like(c_ref)
    acc_ref[...] = jnp.zeros_like(acc_ref)

    def chunk(j, masked):
        off = pl.multiple_of(j * blk, blk)
        k = k_ref[0, pl.ds(off, blk), :]
        z = _nt_dot(k, q) * (SB_DH ** -0.5)
        sp = _softplus(z)
        lk = -sp
        lb = z - sp
        if masked:
            kpos = j * blk + lax.broadcasted_iota(jnp.int32, (blk, 1), 0)
            vis = kpos < qpos
            lk = jnp.where(vis, lk, 0.0)
        hi = lk.astype(BF16)
        lo = (lk - hi.astype(F32)).astype(BF16)
        t = _dot(later, jnp.concatenate([hi, lo], axis=1))
        c = c_ref[...]
        a = jnp.exp(lb + (t[:, :blk] + t[:, blk:]) + c)
        if masked:
            a = jnp.where(vis, a, 0.0)
        acc_ref[...] += _dot(vt_ref[0, j], a.astype(BF16))
        c_new = c + jnp.sum(lk, axis=0, keepdims=True)
        c_ref[...] = c_new
        return jnp.max(c_new)

    def cond(carry):
        t, c_max = carry
        return jnp.logical_and(t < i, c_max > EXP_UNDERFLOW)

    def body(carry):
        t, _ = carry
        return t + 1, chunk(i - 1 - t, False)

    lax.while_loop(cond, body, (jnp.int32(0), chunk(i, True)))
    o_ref[...] = acc_ref[...].T


def _sb_prompt(q, k, vt, *, t_valid):
    n_h, t_pad, _ = q.shape
    blk = vt.shape[-1]
    nq = pl.cdiv(t_valid, blk)
    assert nq * blk <= t_pad
    return pl.pallas_call(
        functools.partial(_sb_prompt_kernel, blk=blk),
        out_shape=jax.ShapeDtypeStruct((t_pad, n_h * HEAD_W), F32),
        grid=(n_h, nq),
        in_specs=[pl.BlockSpec((1, blk, HEAD_W), lambda h, i: (h, i, 0)),
                  pl.BlockSpec((1, t_pad, HEAD_W), lambda h, i: (h, 0, 0)),
                  pl.BlockSpec((1, t_pad // blk, HEAD_W, blk), lambda h, i: (h, 0, 0, 0))],
        out_specs=pl.BlockSpec((blk, HEAD_W), lambda h, i: (i, h)),
        scratch_shapes=[pltpu.VMEM((1, blk), F32), pltpu.VMEM((HEAD_W, blk), F32)],
        compiler_params=_cparams(("arbitrary", "arbitrary")),
        name="sb_prompt",
    )(q, k, vt)


def _sb_sample_kernel(pt_ref, q_ref, kn_ref, vn_ref, *refs, n_tok, pages_per_step):
    g_pages = pages_per_step
    k_refs = refs[:g_pages]
    v_refs = refs[g_pages:2 * g_pages]
    o_ref = refs[2 * g_pages]
    q8_ref, c_ref, acc_ref = refs[2 * g_pages + 1:]
    n_h = q_ref.shape[1]
    page = k_refs[0].shape[3]
    rows8 = 8
    p = pl.program_id(1)
    u = _later_keys_matrix(page)

    @pl.when(p == 0)
    def _():
        qrow = lax.broadcasted_iota(jnp.int32, (rows8, 1), 0)
        for h in range(n_h):
            q8 = jnp.concatenate([q_ref[0, h].astype(F32), jnp.zeros((rows8 - n_tok, HEAD_W), F32)], axis=0)
            q8_ref[h] = q8
            kn = _bf16_round(kn_ref[0, h])
            vn = _bf16_round(vn_ref[0, h])
            c = jnp.zeros((rows8, 1), F32)
            acc = jnp.zeros((rows8, HEAD_W), F32)
            for t in range(n_tok - 1, -1, -1):
                z = jnp.sum(q8 * kn[t:t + 1], axis=-1, keepdims=True) * (SB_DH ** -0.5)
                sp = _softplus(z)
                vis = qrow > t
                a = jnp.where(vis, jnp.exp(z - sp + c), 0.0)
                acc = acc + _bf16_round(a) * vn[t:t + 1]
                c = c + jnp.where(vis, -sp, 0.0)
            c_ref[h] = c
            acc_ref[h] = acc

    zs = []
    for h in range(n_h):
        qh = q8_ref[h].astype(BF16)
        zs.append(jnp.concatenate(
            [_nt_dot(qh, k_refs[g][0, 0, h].astype(BF16)) for g in range(g_pages)], axis=0) * (SB_DH ** -0.5))
    lbs, tails, psums = [], [], []
    for h in range(n_h):
        sp = _softplus(zs[h])
        lbs.append(zs[h] - sp)
        tails.append(_tail_sums(-sp, u))
        psums.append(jnp.sum(-sp, axis=-1, keepdims=True))
    for h in range(n_h):
        c = c_ref[h]
        cs = []
        for g in range(g_pages):
            cs.append(c)
            c = c + psums[h][rows8 * g:rows8 * (g + 1)]
        c_ref[h] = c
        a = jnp.exp(lbs[h] + tails[h] + jnp.concatenate(cs, axis=0))
        pv = _dot(a[:rows8].astype(BF16), v_refs[0][0, 0, h].astype(BF16))
        for g in range(1, g_pages):
            pv += _dot(a[rows8 * g:rows8 * (g + 1)].astype(BF16), v_refs[g][0, 0, h].astype(BF16))
        acc_ref[h] += pv

    @pl.when(p == pl.num_programs(1) - 1)
    def _():
        o_ref[0] = jnp.concatenate([acc_ref[h][:n_tok] for h in range(n_h)], axis=1)


def _sb_sample(q, kn, vn, cache_k, cache_v, page_table, layer, *, pages_per_step=PAGES_PER_STEP):
    n_seq, n_h, n_tok, _ = q.shape
    n_pages = page_table.shape[1]
    page = cache_k.shape[3]
    assert n_pages % pages_per_step == 0 and n_tok <= 8
    tok_spec = pl.BlockSpec((1, n_h, n_tok, HEAD_W), lambda b, p, pt: (b, 0, 0, 0))
    pspecs = _page_specs(layer, n_pages, pages_per_step, n_h, page, reverse=True)
    out = pl.pallas_call(
        functools.partial(_sb_sample_kernel, n_tok=n_tok, pages_per_step=pages_per_step),
        out_shape=jax.ShapeDtypeStruct((n_seq, n_tok, n_h * HEAD_W), F32),
        grid_spec=pltpu.PrefetchScalarGridSpec(
            num_scalar_prefetch=1,
            grid=(n_seq, n_pages // pages_per_step),
            in_specs=[tok_spec, tok_spec, tok_spec] + pspecs + pspecs,
            out_specs=pl.BlockSpec((1, n_tok, n_h * HEAD_W), lambda b, p, pt: (b, 0, 0)),
            scratch_shapes=[pltpu.VMEM((n_h, 8, HEAD_W), F32), pltpu.VMEM((n_h, 8, 1), F32),
                            pltpu.VMEM((n_h, 8, HEAD_W), F32)],
        ),
        compiler_params=_cparams(("arbitrary", "arbitrary")),
        name="sb_sample",
    )(page_table.reshape(-1), q, kn, vn, *([cache_k] * pages_per_step), *([cache_v] * pages_per_step))
    return out.reshape(n_seq * n_tok, n_h * HEAD_W)


def _merge_kernel(x_ref, og_ref, gg_ref, od_ref, dg_ref, os_ref, sg_ref, w_ref, o_ref):
    ag = (og_ref[...] * _silu(gg_ref[...])).astype(BF16)
    ad = (od_ref[...] * _silu(dg_ref[...])).astype(BF16)
    asb = (os_ref[...] * _silu(sg_ref[...])).astype(BF16)
    y = _dot(ag, w_ref[0:GLA_WIDTH, :])
    y += _dot(ad, w_ref[GLA_WIDTH:GLA_WIDTH + ATT_WIDTH, :])
    y += _dot(asb, w_ref[GLA_WIDTH + ATT_WIDTH:, :])
    o_ref[...] = x_ref[...] + y


def _merge(x, og, pg, od, dg, osb, sg, w_out, *, tm):
    t_valid, d = x.shape
    t_pad = og.shape[0]
    assert PG_G % GLA_WIDTH == 0
    return pl.pallas_call(
        _merge_kernel,
        out_shape=jax.ShapeDtypeStruct((t_valid, d), F32),
        grid=(t_pad // tm,),
        in_specs=[pl.BlockSpec((tm, d), lambda i: (i, 0)),
                  pl.BlockSpec((tm, GLA_WIDTH), lambda i: (i, 0)),
                  pl.BlockSpec((tm, GLA_WIDTH), lambda i: (i, PG_G // GLA_WIDTH)),
                  pl.BlockSpec((tm, ATT_WIDTH), lambda i: (i, 0)),
                  pl.BlockSpec((tm, ATT_WIDTH), lambda i: (i, 0)),
                  pl.BlockSpec((tm, ATT_WIDTH), lambda i: (i, 0)),
                  pl.BlockSpec((tm, ATT_WIDTH), lambda i: (i, 0)),
                  pl.BlockSpec(w_out.shape, lambda i: (0, 0))],
        out_specs=pl.BlockSpec((tm, d), lambda i: (i, 0)),
        compiler_params=_cparams(("arbitrary",)),
        name="merge",
    )(x, og, pg, od, dg, osb, sg, w_out)


def _final_norm_kernel(x_ref, g_ref, o_ref):
    o_ref[...] = _rms(x_ref[...], g_ref[...])


def _final_norm(x, g, *, tm):
    t, d = x.shape
    return pl.pallas_call(
        _final_norm_kernel,
        out_shape=jax.ShapeDtypeStruct((t, d), F32),
        grid=(pl.cdiv(t, tm),),
        in_specs=[pl.BlockSpec((tm, d), lambda i: (i, 0)), pl.BlockSpec((1, d), lambda i: (0, 0))],
        out_specs=pl.BlockSpec((tm, d), lambda i: (i, 0)),
        compiler_params=_cparams(("arbitrary",)),
        name="final_norm",
    )(x, g.reshape(1, d))


def _head_major_to_cache(a):
    return jnp.transpose(a, (1, 0, 2))


def kernel(x_prompt, x_sample, cache_diff_k, cache_diff_v, cache_sb_k, cache_sb_v, state_gla, page_table, meta_tokens, norm_mix, w_in, gla_w_gate, gla_b_gate, gla_norm, diff_lambda_q1, diff_lambda_k1, diff_lambda_q2, diff_lambda_k2, diff_norm, w_out, final_norm):
    n_batch, seq, d = x_prompt.shape
    assert n_batch == 1
    n_seq, n_tok, _ = x_sample.shape
    depth = w_in.shape[0]
    t_valid = seq + META_LEN
    row_tile = 512
    t_pad = pl.cdiv(t_valid, row_tile) * row_tile
    n_s = n_seq * n_tok

    xp = jnp.concatenate([meta_tokens.astype(x_prompt.dtype), x_prompt[0]], axis=0)
    xs = x_sample.reshape(n_s, d)
    caches = [jnp.transpose(c, (0, 1, 3, 2, 4)) for c in (cache_diff_k, cache_diff_v, cache_sb_k, cache_sb_v)]

    p_rows, s_rows = [], []
    for l in range(depth):
        lam_init = 0.8 - 0.6 * math.exp(-0.3 * l)
        w_packed = _pack_w_in(w_in[l])
        w_out_bf = w_out[l].astype(BF16)
        wg_pad = jnp.zeros((COL_TILE, COL_TILE), F32).at[:GLA_RANK].set(gla_w_gate[l]).astype(BF16)
        bg = gla_b_gate[l].reshape(1, -1)
        gn = gla_norm[l].reshape(1, -1)
        dn = diff_norm[l].reshape(1, -1)
        lqk = jnp.stack([diff_lambda_q1[l], diff_lambda_k1[l], diff_lambda_q2[l], diff_lambda_k2[l]])

        pr = _inproj(xp, norm_mix[l], w_packed, t_pad=t_pad, tm=row_tile,
                     attention_copies={"d": DIFF_KEY_CHUNK, "s": SB_KEY_CHUNK})
        og, s_fin = _gla_prompt(pr["pg"], wg_pad, bg, gn, t_valid=t_valid)
        od = _diff_prompt(pr["dq"], pr["dkb"], pr["dvt"], lqk, dn, t_valid=t_valid, lam_init=lam_init)
        osb = _sb_prompt(pr["sq"], pr["skb"], pr["svt"], t_valid=t_valid)
        xp = _merge(xp, og, pr["pg"], od, pr["dg"], osb, pr["sg"], w_out_bf, tm=row_tile)
        p_rows.append(tuple(_head_major_to_cache(pr[k])[None] for k in ("dk", "dv", "sk", "sv")) + (s_fin[None],))

        sr = _inproj(xs, norm_mix[l], w_packed, t_pad=n_s, tm=n_s, attention_copies=None)
        og_s, s_new = _gla_sample(sr["pg"], state_gla, l, wg_pad, bg, gn, n_seq=n_seq, n_tok=n_tok)
        per_seq = lambda a: a.reshape(a.shape[0], n_seq, n_tok, HEAD_W).transpose(1, 0, 2, 3)
        od_s = _diff_sample(per_seq(sr["dq"]), per_seq(sr["dk"]), per_seq(sr["dv"]), caches[0], caches[1],
                            page_table, l, lqk, dn, lam_init=lam_init)
        os_s = _sb_sample(per_seq(sr["sq"]), per_seq(sr["sk"]), per_seq(sr["sv"]), caches[2], caches[3],
                          page_table, l)
        xs = _merge(xs, og_s, sr["pg"], od_s, sr["dg"], os_s, sr["sg"], w_out_bf, tm=n_s)
        to_cache = lambda a: a.reshape(a.shape[0], n_seq, n_tok, HEAD_W).transpose(1, 2, 0, 3)
        s_rows.append(tuple(to_cache(sr[k]) for k in ("dk", "dv", "sk", "sv")) + (s_new,))

    y_prompt = _final_norm(xp[META_LEN:], final_norm, tm=row_tile)[None]
    y_sample = _final_norm(xs, final_norm, tm=n_s).reshape(n_seq, n_tok, d)
    stack = lambda rows, i: jnp.stack([r[i] for r in rows])
    return (y_prompt, y_sample,
            stack(p_rows, 0), stack(p_rows, 1), stack(p_rows, 2), stack(p_rows, 3), stack(p_rows, 4),
            stack(s_rows, 0), stack(s_rows, 1), stack(s_rows, 2), stack(s_rows, 3), stack(s_rows, 4))
```

```python
import functools
import math

import jax
import jax.numpy as jnp
from jax import lax
from jax.experimental import pallas as pl
from jax.experimental.pallas import tpu as pltpu

F32 = jnp.float32
BF16 = jnp.bfloat16

META_LEN = 16
GLA_HEADS = 4
GLA_DK = 64
GLA_DV = 128
GLA_RANK = 16
GLA_TAU = 16.0
DIFF_HEADS = 6
DIFF_DH = 64
SB_HEADS = 6
SB_DH = 128
HEAD_W = 128
EPS = 1e-6
NEG = -1e30
EXP_UNDERFLOW = -110.0

GLA_WIDTH = GLA_HEADS * GLA_DV
ATT_WIDTH = DIFF_HEADS * HEAD_W
IN_WIDTHS = (256, 256, 512, 16, 512, 768, 768, 768, 768, 768, 768, 768, 768)

COL_TILE = 256
PG_WIDTH = 7 * COL_TILE
PG_Q, PG_K, PG_V, PG_G, PG_R = 0, 256, 512, 1024, 1536
SEGMENTS = (("pg", 7), ("dq", 3), ("dk", 3), ("dv", 3), ("dg", 3), ("sq", 3), ("sk", 3), ("sv", 3), ("sg", 3))
N_COL_TILES = sum(n for _, n in SEGMENTS)

VMEM_LIMIT = 56 * 1024 * 1024
DIFF_KEY_CHUNK = 512
SB_KEY_CHUNK = 256
PAGES_PER_STEP = 8


def _cparams(sem):
    return pltpu.CompilerParams(dimension_semantics=sem, vmem_limit_bytes=VMEM_LIMIT)


def _softplus(z):
    return jnp.maximum(z, 0.0) + jnp.log1p(jnp.exp(-jnp.abs(z)))


def _log_sigmoid(z):
    return -_softplus(-z)


def _silu(g):
    return g / (1.0 + jnp.exp(-g))


def _nt_dot(a, b):
    return lax.dot_general(a, b, (((1,), (1,)), ((), ())), preferred_element_type=F32)


def _dot(a, b):
    return jnp.dot(a, b, preferred_element_type=F32)


def _rms(o, g):
    return o * lax.rsqrt(jnp.mean(o * o, axis=-1, keepdims=True) + EPS) * g


def _pack_w_in(w):
    offs = [0]
    for wd in IN_WIDTHS:
        offs.append(offs[-1] + wd)
    c = [w[:, offs[i]:offs[i + 1]] for i in range(len(IN_WIDTHS))]
    gq, gk, gv, gr, gg = c[:5]
    pad = jnp.zeros((w.shape[0], PG_WIDTH - PG_R - GLA_RANK), w.dtype)
    packed = jnp.concatenate([gq, gk, gv, gg, gr, pad] + c[5:], axis=1).astype(BF16)
    return packed.reshape(w.shape[0], N_COL_TILES, COL_TILE).transpose(1, 0, 2)


def _inproj_kernel(x_ref, g_ref, w_ref, *refs, t_valid, tm, outs):
    out_refs = refs[:len(outs)]
    h_ref = refs[len(outs)]
    i = pl.program_id(0)
    j = pl.program_id(1)

    @pl.when(j == 0)
    def _():
        x = x_ref[...]
        y = x * lax.rsqrt(jnp.mean(x * x, axis=-1, keepdims=True) + EPS) * g_ref[...]
        rows = i * tm + lax.broadcasted_iota(jnp.int32, (tm, 1), 0)
        h_ref[...] = jnp.where(rows < t_valid, y, 0.0).astype(BF16)

    segments = sorted({(start, n) for start, n, _ in outs})
    for start, n in segments:
        @pl.when((j >= start) & (j < start + n))
        def _(start=start, n=n):
            acc = _dot(h_ref[...], w_ref[0])
            for o_ref, (o_start, _, kind) in zip(out_refs, outs):
                if o_start != start:
                    continue
                if kind == "rows":
                    o_ref[...] = acc.astype(o_ref.dtype)
                    continue
                for hh in range(2):
                    a = acc[:, HEAD_W * hh:HEAD_W * (hh + 1)]
                    if kind == "heads":
                        o_ref[hh] = a.astype(o_ref.dtype)
                    else:
                        at = a.T
                        tw = o_ref.shape[-1]
                        for c in range(tm // tw):
                            o_ref[hh, c] = at[:, c * tw:(c + 1) * tw].astype(o_ref.dtype)


def _inproj(x, g, w_packed, *, t_pad, tm, attention_copies):
    t_valid, d = x.shape
    assert t_pad % tm == 0
    starts = {}
    s = 0
    for name, n in SEGMENTS:
        starts[name] = (s, n)
        s += n
    spec = [("pg", "pg", "rows", F32, t_pad, 0)]
    for a in ("d", "s"):
        spec += [(a + "q", a + "q", "heads", BF16, t_pad, 0),
                 (a + "k", a + "k", "heads", F32, t_valid, 0),
                 (a + "v", a + "v", "heads", F32, t_valid, 0),
                 (a + "g", a + "g", "rows", F32, t_pad, 0)]
        if attention_copies is not None:
            assert tm % attention_copies[a] == 0
            spec += [(a + "kb", a + "k", "heads", BF16, t_pad, 0),
                     (a + "vt", a + "v", "heads_t", BF16, t_pad, attention_copies[a])]
    out_shapes, out_specs, outs = [], [], []
    for key, seg, kind, dt, rows, tw in spec:
        start, n = starts[seg]
        outs.append((start, n, kind))
        if kind == "heads":
            out_shapes.append(jax.ShapeDtypeStruct((2 * n, rows, HEAD_W), dt))
            out_specs.append(pl.BlockSpec(
                (2, tm, HEAD_W), lambda i, j, start=start, n=n: (jnp.clip(j - start, 0, n - 1), i, 0)))
        elif kind == "heads_t":
            out_shapes.append(jax.ShapeDtypeStruct((2 * n, rows // tw, HEAD_W, tw), dt))
            out_specs.append(pl.BlockSpec(
                (2, tm // tw, HEAD_W, tw), lambda i, j, start=start, n=n: (jnp.clip(j - start, 0, n - 1), i, 0, 0)))
        else:
            out_shapes.append(jax.ShapeDtypeStruct((rows, n * COL_TILE), dt))
            out_specs.append(pl.BlockSpec(
                (tm, COL_TILE), lambda i, j, start=start, n=n: (i, jnp.clip(j - start, 0, n - 1))))
    res = pl.pallas_call(
        functools.partial(_inproj_kernel, t_valid=t_valid, tm=tm, outs=tuple(outs)),
        out_shape=out_shapes,
        grid=(t_pad // tm, N_COL_TILES),
        in_specs=[pl.BlockSpec((tm, d), lambda i, j: (i, 0)),
                  pl.BlockSpec((1, d), lambda i, j: (0, 0)),
                  pl.BlockSpec((1, d, COL_TILE), lambda i, j: (j, 0, 0))],
        out_specs=out_specs,
        scratch_shapes=[pltpu.VMEM((tm, d), BF16)],
        compiler_params=_cparams(("arbitrary", "arbitrary")),
        name="inproj",
    )(x, g.reshape(1, d), w_packed)
    return {key: r for (key, *_), r in zip(spec, res)}


def _cumsum_rows(lg, n):
    ri = lax.broadcasted_iota(jnp.int32, (n, n), 0)
    ci = lax.broadcasted_iota(jnp.int32, (n, n), 1)
    tri = jnp.where(ri >= ci, 1.0, 0.0).astype(BF16)
    hi = lg.astype(BF16)
    lo = (lg - hi.astype(F32)).astype(BF16)
    return _dot(tri, hi) + _dot(tri, lo)


def _gla_prompt_kernel(pg_ref, wg_ref, bg_ref, gn_ref, og_ref, sfin_ref, s_ref, *, t_valid, chunk, sub):
    step = pl.program_id(0)

    @pl.when(step == 0)
    def _():
        s_ref[...] = jnp.zeros_like(s_ref)

    rows = step * chunk + lax.broadcasted_iota(jnp.int32, (chunk, 1), 0)
    x = _dot(pg_ref[:, PG_R:PG_R + COL_TILE].astype(BF16), wg_ref[...]) + bg_ref[...]
    lg = jnp.where(rows < t_valid, _log_sigmoid(x) * (1.0 / GLA_TAU), 0.0)
    b = _cumsum_rows(lg, chunk)

    lane = lax.broadcasted_iota(jnp.int32, (1, HEAD_W), 1)
    head_mask = (lane < GLA_DK, lane >= GLA_DK)
    n_sub = chunk // sub
    ri = lax.broadcasted_iota(jnp.int32, (sub, chunk), 0)
    ci = lax.broadcasted_iota(jnp.int32, (sub, chunk), 1)
    krow = lax.broadcasted_iota(jnp.int32, (chunk, 1), 0)

    pair_work = []
    for p in range(2):
        qp = pg_ref[:, PG_Q + HEAD_W * p:PG_Q + HEAD_W * (p + 1)] * (GLA_DK ** -0.5)
        kp = pg_ref[:, PG_K + HEAD_W * p:PG_K + HEAD_W * (p + 1)]
        bp = b[:, HEAD_W * p:HEAD_W * (p + 1)]
        s_old = s_ref[p]
        s_bf = s_old.astype(BF16)
        qe = qp * jnp.exp(bp)
        b_end = bp[chunk - 1:chunk]
        kend_t = (kp * jnp.exp(b_end - bp)).T
        dec_t = jnp.broadcast_to(jnp.exp(b_end), (HEAD_W, HEAD_W)).T
        q_blk, k_blk = [], []
        for blk in range(n_sub):
            lo, hi = blk * sub, (blk + 1) * sub
            r = jnp.zeros((1, HEAD_W), F32) if blk == 0 else bp[lo - 1:lo]
            q_blk.append(qp[lo:hi] * jnp.exp(bp[lo:hi] - r))
            k_blk.append((kp * jnp.exp(jnp.where(krow < hi, r - bp, 0.0))).astype(BF16))
        upd, o_inter, att_raw, vbs = [], [], [], []
        for hh in range(2):
            h = 2 * p + hh
            vb = pg_ref[:, PG_V + GLA_DV * h:PG_V + GLA_DV * (h + 1)].astype(BF16)
            vbs.append(vb)
            o_inter.append(_dot(jnp.where(head_mask[hh], qe, 0.0).astype(BF16), s_bf))
            upd.append(_dot(kend_t[GLA_DK * hh:GLA_DK * (hh + 1)].astype(BF16), vb))
            att_raw.append([_nt_dot(jnp.where(head_mask[hh], q_blk[blk], 0.0).astype(BF16), k_blk[blk])
                            for blk in range(n_sub)])
        s_ref[p] = dec_t * s_old + jnp.concatenate(upd, axis=0)
        pair_work.append((o_inter, att_raw, vbs))

    for p in range(2):
        o_inter, att_raw, vbs = pair_work[p]
        for hh in range(2):
            h = 2 * p + hh
            att = [jnp.where(ci <= ri + blk * sub, att_raw[hh][blk], 0.0).astype(BF16) for blk in range(n_sub)]
            o = o_inter[hh] + jnp.concatenate([_dot(a, vbs[hh]) for a in att], axis=0)
            og_ref[:, GLA_DV * h:GLA_DV * (h + 1)] = _rms(o, gn_ref[...])

    @pl.when(step == pl.num_programs(0) - 1)
    def _():
        sfin_ref[...] = s_ref[...]


def _gla_prompt(pg, wg_pad, bg, gn, *, t_valid, chunk=128, sub=16):
    t_pad = pg.shape[0]
    n_steps = pl.cdiv(t_valid, chunk)
    assert n_steps * chunk <= t_pad
    og, sfin = pl.pallas_call(
        functools.partial(_gla_prompt_kernel, t_valid=t_valid, chunk=chunk, sub=sub),
        out_shape=[jax.ShapeDtypeStruct((n_steps * chunk, GLA_WIDTH), F32),
                   jax.ShapeDtypeStruct((2, HEAD_W, GLA_DV), F32)],
        grid=(n_steps,),
        in_specs=[pl.BlockSpec((chunk, PG_WIDTH), lambda i: (i, 0)),
                  pl.BlockSpec((COL_TILE, COL_TILE), lambda i: (0, 0)),
                  pl.BlockSpec((1, COL_TILE), lambda i: (0, 0)),
                  pl.BlockSpec((1, GLA_DV), lambda i: (0, 0))],
        out_specs=[pl.BlockSpec((chunk, GLA_WIDTH), lambda i: (i, 0)),
                   pl.BlockSpec((2, HEAD_W, GLA_DV), lambda i: (0, 0, 0))],
        scratch_shapes=[pltpu.VMEM((2, HEAD_W, GLA_DV), F32)],
        compiler_params=_cparams(("arbitrary",)),
        name="gla_prompt",
    )(pg, wg_pad, bg, gn)
    return og, sfin.reshape(GLA_HEADS, GLA_DK, GLA_DV)


def _gla_sample_kernel(pg_ref, st_ref, wg_ref, bg_ref, gn_ref, og_ref, snew_ref, *, n_tok):
    pad = HEAD_W - n_tok
    zpad = jnp.zeros((pad, HEAD_W), F32)
    gr = jnp.concatenate([pg_ref[0, :,PG_R:PG_R + COL_TILE], jnp.zeros((pad, COL_TILE), F32)], axis=0)
    x = _dot(gr.astype(BF16), wg_ref[...]) + bg_ref[...]
    rows = lax.broadcasted_iota(jnp.int32, (HEAD_W, 1), 0)
    lg = jnp.where(rows < n_tok, _log_sigmoid(x) * (1.0 / GLA_TAU), 0.0)
    b = _cumsum_rows(lg, HEAD_W)

    lane = lax.broadcasted_iota(jnp.int32, (1, HEAD_W), 1)
    head_mask = (lane < GLA_DK, lane >= GLA_DK)
    ri = lax.broadcasted_iota(jnp.int32, (HEAD_W, HEAD_W), 0)
    ci = lax.broadcasted_iota(jnp.int32, (HEAD_W, HEAD_W), 1)
    og = []
    for p in range(2):
        qp = jnp.concatenate([pg_ref[0, :,PG_Q + HEAD_W * p:PG_Q + HEAD_W * (p + 1)], zpad], axis=0) * (GLA_DK ** -0.5)
        kp = jnp.concatenate([pg_ref[0, :,PG_K + HEAD_W * p:PG_K + HEAD_W * (p + 1)], zpad], axis=0)
        bp = b[:, HEAD_W * p:HEAD_W * (p + 1)]
        s_old = jnp.concatenate([st_ref[0, 0, 2 * p], st_ref[0, 0, 2 * p + 1]], axis=0)
        s_bf = s_old.astype(BF16)
        qe = qp * jnp.exp(bp)
        kinv = (kp * jnp.exp(-bp)).astype(BF16)
        b_end = bp[HEAD_W - 1:HEAD_W]
        kend_t = (kp * jnp.exp(b_end - bp)).T
        dec_t = jnp.broadcast_to(jnp.exp(b_end), (HEAD_W, HEAD_W)).T
        upd = []
        for hh in range(2):
            h = 2 * p + hh
            vb = jnp.concatenate([pg_ref[0, :,PG_V + GLA_DV * h:PG_V + GLA_DV * (h + 1)], zpad], axis=0).astype(BF16)
            qm = jnp.where(head_mask[hh], qe, 0.0).astype(BF16)
            att = jnp.where(ci <= ri, _nt_dot(qm, kinv), 0.0)
            o = _dot(qm, s_bf) + _dot(att.astype(BF16), vb)
            og.append(_rms(o, gn_ref[...])[:n_tok])
            upd.append(_dot(kend_t[GLA_DK * hh:GLA_DK * (hh + 1)].astype(BF16), vb))
        s_new = dec_t * s_old + jnp.concatenate(upd, axis=0)
        snew_ref[0, 2 * p] = s_new[:GLA_DK]
        snew_ref[0, 2 * p + 1] = s_new[GLA_DK:]
    og_ref[0] = jnp.concatenate(og, axis=1)


def _gla_sample(pg, state, layer, wg_pad, bg, gn, *, n_seq, n_tok):
    pg3 = pg.reshape(n_seq, n_tok, PG_WIDTH)
    og, snew = pl.pallas_call(
        functools.partial(_gla_sample_kernel, n_tok=n_tok),
        out_shape=[jax.ShapeDtypeStruct((n_seq, n_tok, GLA_WIDTH), F32),
                   jax.ShapeDtypeStruct((n_seq, GLA_HEADS, GLA_DK, GLA_DV), F32)],
        grid=(n_seq,),
        in_specs=[pl.BlockSpec((1, n_tok, PG_WIDTH), lambda i: (i, 0, 0)),
                  pl.BlockSpec((1, 1, GLA_HEADS, GLA_DK, GLA_DV), lambda i: (layer, i, 0, 0, 0)),
                  pl.BlockSpec((COL_TILE, COL_TILE), lambda i: (0, 0)),
                  pl.BlockSpec((1, COL_TILE), lambda i: (0, 0)),
                  pl.BlockSpec((1, GLA_DV), lambda i: (0, 0))],
        out_specs=[pl.BlockSpec((1, n_tok, GLA_WIDTH), lambda i: (i, 0, 0)),
                   pl.BlockSpec((1, GLA_HEADS, GLA_DK, GLA_DV), lambda i: (i, 0, 0, 0))],
        compiler_params=_cparams(("arbitrary",)),
        name="gla_sample",
    )(pg3, state, wg_pad, bg, gn)
    return og.reshape(n_seq * n_tok, GLA_WIDTH), snew


def _diff_lambda(lqk_ref, lam_init):
    s1 = jnp.sum(lqk_ref[0:1, :] * lqk_ref[1:2, :], axis=-1, keepdims=True)
    s2 = jnp.sum(lqk_ref[2:3, :] * lqk_ref[3:4, :], axis=-1, keepdims=True)
    return jnp.exp(s1) - jnp.exp(s2) + lam_init


def _split_maps(q):
    lane = lax.broadcasted_iota(jnp.int32, q.shape, 1)
    zero = jnp.zeros_like(q)
    return jnp.concatenate([jnp.where(lane < DIFF_DH, q, zero), jnp.where(lane >= DIFF_DH, q, zero)], axis=0)


def _diff_prompt_kernel(lqk_ref, dn_ref, q_ref, k_ref, vt_ref, o_ref, m_ref, l_ref, acc_ref, s_ref, *, bq, bk,
                        lam_init):
    i = pl.program_id(1)
    qq = _split_maps(q_ref[0]) * (DIFF_DH ** -0.5)
    lane = lax.broadcasted_iota(jnp.int32, (1, 2 * bq), 1)
    qpos = i * bq + jnp.where(lane < bq, lane, lane - bq)
    m_ref[...] = jnp.full_like(m_ref, NEG)
    l_ref[...] = jnp.zeros_like(l_ref)
    acc_ref[...] = jnp.zeros_like(acc_ref)

    n_full = (i * bq + 1) // bk
    n_need = (i * bq + bq + bk - 1) // bk

    def scores(j):
        off = pl.multiple_of(j * bk, bk)
        return _nt_dot(k_ref[0, pl.ds(off, bk), :], qq)

    s_ref[...] = scores(0)

    def chunk(j, masked):
        s = s_ref[...]
        s_next = scores(jnp.minimum(j + 1, n_need - 1))
        if masked:
            kpos = j * bk + lax.broadcasted_iota(jnp.int32, (bk, 1), 0)
            s = jnp.where(kpos <= qpos, s, NEG)
        m_old = m_ref[...]
        m_new = jnp.maximum(m_old, jnp.max(s, axis=0, keepdims=True))
        alpha = jnp.exp(m_old - m_new)
        p = jnp.exp(s - m_new)
        l_ref[...] = alpha * l_ref[...] + jnp.sum(p, axis=0, keepdims=True)
        acc_ref[...] = alpha * acc_ref[...] + _dot(vt_ref[0, j], p.astype(BF16))
        m_ref[...] = m_new
        s_ref[...] = s_next

    def full_body(j, c):
        chunk(j, False)
        return c

    def diag_body(j, c):
        chunk(j, True)
        return c

    lax.fori_loop(0, n_full, full_body, 0)
    lax.fori_loop(n_full, n_need, diag_body, 0)

    lam = _diff_lambda(lqk_ref, lam_init)
    on = acc_ref[...] / l_ref[...]
    o = (on[:, :bq] - lam * on[:, bq:]).T
    o_ref[...] = _rms(o, dn_ref[...]) * (1.0 - lam_init)


def _diff_prompt(q, k, vt, lqk, dn, *, t_valid, lam_init, bq=256):
    n_h, t_pad, _ = q.shape
    bk = vt.shape[-1]
    nq = pl.cdiv(t_valid, bq)
    assert pl.cdiv(nq * bq, bk) * bk <= t_pad
    return pl.pallas_call(
        functools.partial(_diff_prompt_kernel, bq=bq, bk=bk, lam_init=lam_init),
        out_shape=jax.ShapeDtypeStruct((nq * bq, n_h * HEAD_W), F32),
        grid=(n_h, nq),
        in_specs=[pl.BlockSpec((4, DIFF_DH), lambda h, i: (0, 0)),
                  pl.BlockSpec((1, HEAD_W), lambda h, i: (0, 0)),
                  pl.BlockSpec((1, bq, HEAD_W), lambda h, i: (h, i, 0)),
                  pl.BlockSpec((1, t_pad, HEAD_W), lambda h, i: (h, 0, 0)),
                  pl.BlockSpec((1, t_pad // bk, HEAD_W, bk), lambda h, i: (h, 0, 0, 0))],
        out_specs=pl.BlockSpec((bq, HEAD_W), lambda h, i: (i, h)),
        scratch_shapes=[pltpu.VMEM((1, 2 * bq), F32), pltpu.VMEM((1, 2 * bq), F32),
                        pltpu.VMEM((HEAD_W, 2 * bq), F32), pltpu.VMEM((bk, 2 * bq), F32)],
        compiler_params=_cparams(("arbitrary", "arbitrary")),
        name="diff_prompt",
    )(lqk, dn, q, k, vt)


def _bf16_round(x):
    return x.astype(BF16).astype(F32)


def _diff_sample_kernel(pt_ref, lqk_ref, dn_ref, q_ref, kn_ref, vn_ref, *refs, n_tok, pages_per_step, lam_init):
    g_pages = pages_per_step
    k_refs = refs[:g_pages]
    v_refs = refs[g_pages:2 * g_pages]
    o_ref = refs[2 * g_pages]
    qq_ref, m_ref, l_ref, acc_ref = refs[2 * g_pages + 1:]
    n_h = q_ref.shape[1]
    n2 = 2 * n_tok
    p = pl.program_id(1)

    @pl.when(p == 0)
    def _():
        qrow = lax.broadcasted_iota(jnp.int32, (n2, 1), 0) % n_tok
        for h in range(n_h):
            qq = _split_maps(q_ref[0, h]) * (DIFF_DH ** -0.5)
            qq_ref[h] = qq.astype(F32)
            qf = qq.astype(F32)
            kn = _bf16_round(kn_ref[0, h])
            vn = _bf16_round(vn_ref[0, h])
            s = [jnp.where(qrow >= t, jnp.sum(qf * kn[t:t + 1], axis=-1, keepdims=True), NEG) for t in range(n_tok)]
            m = s[0]
            for t in range(1, n_tok):
                m = jnp.maximum(m, s[t])
            l = jnp.zeros((n2, 1), F32)
            acc = jnp.zeros((n2, HEAD_W), F32)
            for t in range(n_tok):
                pt = jnp.exp(s[t] - m)
                l = l + pt
                acc = acc + _bf16_round(pt) * vn[t:t + 1]
            m_ref[h] = m
            l_ref[h] = l
            acc_ref[h] = acc

    page = k_refs[0].shape[3]
    scores = []
    for h in range(n_h):
        qh = qq_ref[h].astype(BF16)
        scores.append(jnp.concatenate(
            [_nt_dot(qh, k_refs[g][0, 0, h].astype(BF16)) for g in range(g_pages)], axis=1))
    probs, alphas = [], []
    for h in range(n_h):
        s = scores[h]
        m_old = m_ref[h]
        m_new = jnp.maximum(m_old, jnp.max(s, axis=-1, keepdims=True))
        alpha = jnp.exp(m_old - m_new)
        p32 = jnp.exp(s - m_new)
        l_ref[h] = alpha * l_ref[h] + jnp.sum(p32, axis=-1, keepdims=True)
        m_ref[h] = m_new
        probs.append(p32.astype(BF16))
        alphas.append(alpha)
    for h in range(n_h):
        pv = _dot(probs[h][:, :page], v_refs[0][0, 0, h].astype(BF16))
        for g in range(1, g_pages):
            pv += _dot(probs[h][:, g * page:(g + 1) * page], v_refs[g][0, 0, h].astype(BF16))
        acc_ref[h] = alphas[h] * acc_ref[h] + pv

    @pl.when(p == pl.num_programs(1) - 1)
    def _():
        lam = _diff_lambda(lqk_ref, lam_init)
        outs = []
        for h in range(n_h):
            on = acc_ref[h] / l_ref[h]
            o = on[:n_tok] - lam * on[n_tok:]
            outs.append(_rms(o, dn_ref[...]) * (1.0 - lam_init))
        o_ref[0] = jnp.concatenate(outs, axis=1)


def _page_specs(layer, n_pages, pages_per_step, n_h, page, reverse):
    specs = []
    for g in range(pages_per_step):
        def imap(b, p, pt, g=g):
            idx = p * pages_per_step + g
            if reverse:
                idx = n_pages - 1 - idx
            return (layer, pt[b * n_pages + idx], 0, 0, 0)
        specs.append(pl.BlockSpec((1, 1, n_h, page, HEAD_W), imap))
    return specs


def _diff_sample(q, kn, vn, cache_k, cache_v, page_table, layer, lqk, dn, *, lam_init, pages_per_step=PAGES_PER_STEP):
    n_seq, n_h, n_tok, _ = q.shape
    n_pages = page_table.shape[1]
    page = cache_k.shape[3]
    assert n_pages % pages_per_step == 0
    tok_spec = pl.BlockSpec((1, n_h, n_tok, HEAD_W), lambda b, p, pt: (b, 0, 0, 0))
    pspecs = _page_specs(layer, n_pages, pages_per_step, n_h, page, reverse=False)
    out = pl.pallas_call(
        functools.partial(_diff_sample_kernel, n_tok=n_tok, pages_per_step=pages_per_step, lam_init=lam_init),
        out_shape=jax.ShapeDtypeStruct((n_seq, n_tok, n_h * HEAD_W), F32),
        grid_spec=pltpu.PrefetchScalarGridSpec(
            num_scalar_prefetch=1,
            grid=(n_seq, n_pages // pages_per_step),
            in_specs=[pl.BlockSpec((4, DIFF_DH), lambda b, p, pt: (0, 0)),
                      pl.BlockSpec((1, HEAD_W), lambda b, p, pt: (0, 0)),
                      tok_spec, tok_spec, tok_spec] + pspecs + pspecs,
            out_specs=pl.BlockSpec((1, n_tok, n_h * HEAD_W), lambda b, p, pt: (b, 0, 0)),
            scratch_shapes=[pltpu.VMEM((n_h, 2 * n_tok, HEAD_W), F32), pltpu.VMEM((n_h, 2 * n_tok, 1), F32),
                            pltpu.VMEM((n_h, 2 * n_tok, 1), F32), pltpu.VMEM((n_h, 2 * n_tok, HEAD_W), F32)],
        ),
        compiler_params=_cparams(("arbitrary", "arbitrary")),
        name="diff_sample",
    )(page_table.reshape(-1), lqk, dn, q, kn, vn, *([cache_k] * pages_per_step), *([cache_v] * pages_per_step))
    return out.reshape(n_seq * n_tok, n_h * HEAD_W)


def _later_keys_matrix(n):
    ri = lax.broadcasted_iota(jnp.int32, (n, n), 0)
    ci = lax.broadcasted_iota(jnp.int32, (n, n), 1)
    return jnp.where(ri > ci, 1.0, 0.0).astype(BF16)


def _tail_sums(lk, u):
    n = lk.shape[0]
    hi = lk.astype(BF16)
    lo = (lk - hi.astype(F32)).astype(BF16)
    t = _dot(jnp.concatenate([hi, lo], axis=0), u)
    return t[:n] + t[n:]


def _sb_prompt_kernel(q_ref, k_ref, vt_ref, o_ref, c_ref, acc_ref, *, blk):
    i = pl.program_id(1)
    q = q_ref[0]
    ri = lax.broadcasted_iota(jnp.int32, (blk, blk), 0)
    ci = lax.broadcasted_iota(jnp.int32, (blk, blk), 1)
    later = jnp.where(ci > ri, 1.0, 0.0).astype(BF16)
    qpos = i * blk + lax.broadcasted_iota(jnp.int32, (1, blk), 1)
    c_ref[...] = jnp.zeros_like(c_ref)
    acc_ref[...] = jnp.zeros_like(acc_ref)

    def chunk(j, masked):
        off = pl.multiple_of(j * blk, blk)
        k = k_ref[0, pl.ds(off, blk), :]
        z = _nt_dot(k, q) * (SB_DH ** -0.5)
        sp = _softplus(z)
        lk = -sp
        lb = z - sp
        if masked:
            kpos = j * blk + lax.broadcasted_iota(jnp.int32, (blk, 1), 0)
            vis = kpos < qpos
            lk = jnp.where(vis, lk, 0.0)
        hi = lk.astype(BF16)
        lo = (lk - hi.astype(F32)).astype(BF16)
        t = _dot(later, jnp.concatenate([hi, lo], axis=1))
        c = c_ref[...]
        a = jnp.exp(lb + (t[:, :blk] + t[:, blk:]) + c)
        if masked:
            a = jnp.where(vis, a, 0.0)
        acc_ref[...] += _dot(vt_ref[0, j], a.astype(BF16))
        c_new = c + jnp.sum(lk, axis=0, keepdims=True)
        c_ref[...] = c_new
        return jnp.max(c_new)

    def cond(carry):
        t, c_max = carry
        return jnp.logical_and(t < i, c_max > EXP_UNDERFLOW)

    def body(carry):
        t, _ = carry
        return t + 1, chunk(i - 1 - t, False)

    lax.while_loop(cond, body, (jnp.int32(0), chunk(i, True)))
    o_ref[...] = acc_ref[...].T


def _sb_prompt(q, k, vt, *, t_valid):
    n_h, t_pad, _ = q.shape
    blk = vt.shape[-1]
    nq = pl.cdiv(t_valid, blk)
    assert nq * blk <= t_pad
    return pl.pallas_call(
        functools.partial(_sb_prompt_kernel, blk=blk),
        out_shape=jax.ShapeDtypeStruct((nq * blk, n_h * HEAD_W), F32),
        grid=(n_h, nq),
        in_specs=[pl.BlockSpec((1, blk, HEAD_W), lambda h, i: (h, i, 0)),
                  pl.BlockSpec((1, t_pad, HEAD_W), lambda h, i: (h, 0, 0)),
                  pl.BlockSpec((1, t_pad // blk, HEAD_W, blk), lambda h, i: (h, 0, 0, 0))],
        out_specs=pl.BlockSpec((blk, HEAD_W), lambda h, i: (i, h)),
        scratch_shapes=[pltpu.VMEM((1, blk), F32), pltpu.VMEM((HEAD_W, blk), F32)],
        compiler_params=_cparams(("arbitrary", "arbitrary")),
        name="sb_prompt",
    )(q, k, vt)


def _sb_new_tokens(q8, kn, vn, n_tok):
    rows8 = q8.shape[0]
    qrow = lax.broadcasted_iota(jnp.int32, (rows8, 1), 0)
    c = jnp.zeros((rows8, 1), F32)
    acc = jnp.zeros((rows8, HEAD_W), F32)
    for t in range(n_tok - 1, -1, -1):
        z = jnp.sum(q8 * kn[t:t + 1], axis=-1, keepdims=True) * (SB_DH ** -0.5)
        sp = _softplus(z)
        vis = qrow > t
        a = jnp.where(vis, jnp.exp(z - sp + c), 0.0)
        acc = acc + _bf16_round(a) * vn[t:t + 1]
        c = c + jnp.where(vis, -sp, 0.0)
    return c, acc


def _sb_sample_kernel(pt_ref, done_ref, q_ref, *refs, n_tok, pages_per_step, first_stage):
    g_pages = pages_per_step
    seed_a, seed_b = refs[:2]
    k_refs = refs[2:2 + g_pages]
    v_refs = refs[2 + g_pages:2 + 2 * g_pages]
    n_out = 3 if first_stage else 1
    out_refs = refs[2 + 2 * g_pages:2 + 2 * g_pages + n_out]
    c_ref, acc_ref = refs[2 + 2 * g_pages + n_out:]
    n_h = q_ref.shape[1]
    page = k_refs[0].shape[3]
    rows8 = 8
    b = pl.program_id(0)
    p = pl.program_id(1)
    q8s = [jnp.concatenate([q_ref[0, h].astype(F32), jnp.zeros((rows8 - n_tok, HEAD_W), F32)], axis=0)
           for h in range(n_h)]

    @pl.when(p == 0)
    def _():
        for h in range(n_h):
            if first_stage:
                c, acc = _sb_new_tokens(q8s[h], _bf16_round(seed_a[0, h]), _bf16_round(seed_b[0, h]), n_tok)
            else:
                c, acc = seed_a[0, h], seed_b[0, h]
            c_ref[h] = c
            acc_ref[h] = acc

    def visit_pages():
        _sb_visit_pages(q8s, k_refs, v_refs, c_ref, acc_ref, page)

    if first_stage:
        visit_pages()
    else:
        pl.when(done_ref[b] == 0)(visit_pages)

    @pl.when(p == pl.num_programs(1) - 1)
    def _():
        if first_stage:
            c_out, acc_out, done_out = out_refs
            c_max = jnp.full((1, 1), NEG, F32)
            for h in range(n_h):
                c_out[0, h] = c_ref[h]
                acc_out[0, h] = acc_ref[h]
                c_max = jnp.maximum(c_max, jnp.max(c_ref[h][:n_tok], axis=0, keepdims=True))
            done_out[0] = jnp.where(c_max < EXP_UNDERFLOW, 1, 0).astype(jnp.int32)
        else:
            out_refs[0][0] = jnp.concatenate([acc_ref[h][:n_tok] for h in range(n_h)], axis=1)


def _sb_visit_pages(q8s, k_refs, v_refs, c_ref, acc_ref, page):
    g_pages = len(k_refs)
    n_h = len(q8s)
    rows8 = q8s[0].shape[0]
    u = _later_keys_matrix(page)
    zs = []
    for h in range(n_h):
        qh = q8s[h].astype(BF16)
        zs.append(jnp.concatenate(
            [_nt_dot(qh, k_refs[g][0, 0, h].astype(BF16)) for g in range(g_pages)], axis=0) * (SB_DH ** -0.5))
    lbs, tails, psums = [], [], []
    for h in range(n_h):
        sp = _softplus(zs[h])
        lbs.append(zs[h] - sp)
        tails.append(_tail_sums(-sp, u))
        psums.append(jnp.sum(-sp, axis=-1, keepdims=True))
    for h in range(n_h):
        c = c_ref[h]
        cs = []
        for g in range(g_pages):
            cs.append(c)
            c = c + psums[h][rows8 * g:rows8 * (g + 1)]
        c_ref[h] = c
        a = jnp.exp(lbs[h] + tails[h] + jnp.concatenate(cs, axis=0))
        pv = _dot(a[:rows8].astype(BF16), v_refs[0][0, 0, h].astype(BF16))
        for g in range(1, g_pages):
            pv += _dot(a[rows8 * g:rows8 * (g + 1)].astype(BF16), v_refs[g][0, 0, h].astype(BF16))
        acc_ref[h] += pv


def _sb_sample(q, kn, vn, cache_k, cache_v, page_table, layer, *, pages_per_step=PAGES_PER_STEP):
    n_seq, n_h, n_tok, _ = q.shape
    n_pages = page_table.shape[1]
    page = cache_k.shape[3]
    g_pages = pages_per_step
    assert n_pages % g_pages == 0 and n_pages > g_pages and n_tok <= 8
    pt = page_table.reshape(-1)
    tok_spec = pl.BlockSpec((1, n_h, n_tok, HEAD_W), lambda b, p, pt, done: (b, 0, 0, 0))
    c_spec = pl.BlockSpec((1, n_h, 8, 1), lambda b, p, pt, done: (b, 0, 0, 0))
    acc_spec = pl.BlockSpec((1, n_h, 8, HEAD_W), lambda b, p, pt, done: (b, 0, 0, 0))
    scratch = [pltpu.VMEM((n_h, 8, 1), F32), pltpu.VMEM((n_h, 8, HEAD_W), F32)]

    def page_specs(first_idx, skip_done):
        specs = []
        for g in range(g_pages):
            def imap(b, p, pt, done, g=g):
                phys = pt[b * n_pages + (n_pages - 1 - (first_idx + p * g_pages + g))]
                if skip_done:
                    phys = jnp.where(done[b] == 0, phys, 0)
                return (layer, phys, 0, 0, 0)
            specs.append(pl.BlockSpec((1, 1, n_h, page, HEAD_W), imap))
        return specs

    specs1 = page_specs(0, False)
    c1, acc1, done = pl.pallas_call(
        functools.partial(_sb_sample_kernel, n_tok=n_tok, pages_per_step=g_pages, first_stage=True),
        out_shape=[jax.ShapeDtypeStruct((n_seq, n_h, 8, 1), F32),
                   jax.ShapeDtypeStruct((n_seq, n_h, 8, HEAD_W), F32),
                   jax.ShapeDtypeStruct((n_seq, 1, 1), jnp.int32)],
        grid_spec=pltpu.PrefetchScalarGridSpec(
            num_scalar_prefetch=2,
            grid=(n_seq, 1),
            in_specs=[tok_spec, tok_spec, tok_spec] + specs1 + specs1,
            out_specs=[c_spec, acc_spec, pl.BlockSpec((1, 1, 1), lambda b, p, pt, done: (b, 0, 0))],
            scratch_shapes=scratch,
        ),
        compiler_params=_cparams(("arbitrary", "arbitrary")),
        name="sb_sample_newest",
    )(pt, jnp.zeros((n_seq,), jnp.int32), q, kn, vn, *([cache_k] * g_pages), *([cache_v] * g_pages))

    specs2 = page_specs(g_pages, True)
    out = pl.pallas_call(
        functools.partial(_sb_sample_kernel, n_tok=n_tok, pages_per_step=g_pages, first_stage=False),
        out_shape=jax.ShapeDtypeStruct((n_seq, n_tok, n_h * HEAD_W), F32),
        grid_spec=pltpu.PrefetchScalarGridSpec(
            num_scalar_prefetch=2,
            grid=(n_seq, n_pages // g_pages - 1),
            in_specs=[tok_spec, c_spec, acc_spec] + specs2 + specs2,
            out_specs=pl.BlockSpec((1, n_tok, n_h * HEAD_W), lambda b, p, pt, done: (b, 0, 0)),
            scratch_shapes=scratch,
        ),
        compiler_params=_cparams(("arbitrary", "arbitrary")),
        name="sb_sample_rest",
    )(pt, done.reshape(-1), q, c1, acc1, *([cache_k] * g_pages), *([cache_v] * g_pages))
    return out.reshape(n_seq * n_tok, n_h * HEAD_W)


def _merge_kernel(x_ref, og_ref, gg_ref, od_ref, dg_ref, os_ref, sg_ref, w_ref, o_ref):
    ag = (og_ref[...] * _silu(gg_ref[...])).astype(BF16)
    ad = (od_ref[...] * _silu(dg_ref[...])).astype(BF16)
    asb = (os_ref[...] * _silu(sg_ref[...])).astype(BF16)
    y = _dot(ag, w_ref[0:GLA_WIDTH, :])
    y += _dot(ad, w_ref[GLA_WIDTH:GLA_WIDTH + ATT_WIDTH, :])
    y += _dot(asb, w_ref[GLA_WIDTH + ATT_WIDTH:, :])
    o_ref[...] = x_ref[...] + y


def _merge(x, og, pg, od, dg, osb, sg, w_out, *, tm):
    t_valid, d = x.shape
    assert PG_G % GLA_WIDTH == 0
    return pl.pallas_call(
        _merge_kernel,
        out_shape=jax.ShapeDtypeStruct((t_valid, d), F32),
        grid=(pl.cdiv(t_valid, tm),),
        in_specs=[pl.BlockSpec((tm, d), lambda i: (i, 0)),
                  pl.BlockSpec((tm, GLA_WIDTH), lambda i: (i, 0)),
                  pl.BlockSpec((tm, GLA_WIDTH), lambda i: (i, PG_G // GLA_WIDTH)),
                  pl.BlockSpec((tm, ATT_WIDTH), lambda i: (i, 0)),
                  pl.BlockSpec((tm, ATT_WIDTH), lambda i: (i, 0)),
                  pl.BlockSpec((tm, ATT_WIDTH), lambda i: (i, 0)),
                  pl.BlockSpec((tm, ATT_WIDTH), lambda i: (i, 0)),
                  pl.BlockSpec(w_out.shape, lambda i: (0, 0))],
        out_specs=pl.BlockSpec((tm, d), lambda i: (i, 0)),
        compiler_params=_cparams(("arbitrary",)),
        name="merge",
    )(x, og, pg, od, dg, osb, sg, w_out)


def _final_norm_kernel(x_ref, g_ref, o_ref):
    o_ref[...] = _rms(x_ref[...], g_ref[...])


def _final_norm(x, g, *, tm):
    t, d = x.shape
    return pl.pallas_call(
        _final_norm_kernel,
        out_shape=jax.ShapeDtypeStruct((t, d), F32),
        grid=(pl.cdiv(t, tm),),
        in_specs=[pl.BlockSpec((tm, d), lambda i: (i, 0)), pl.BlockSpec((1, d), lambda i: (0, 0))],
        out_specs=pl.BlockSpec((tm, d), lambda i: (i, 0)),
        compiler_params=_cparams(("arbitrary",)),
        name="final_norm",
    )(x, g.reshape(1, d))


def _head_major_to_cache(a):
    return jnp.transpose(a, (1, 0, 2))


def kernel(x_prompt, x_sample, cache_diff_k, cache_diff_v, cache_sb_k, cache_sb_v, state_gla, page_table, meta_tokens, norm_mix, w_in, gla_w_gate, gla_b_gate, gla_norm, diff_lambda_q1, diff_lambda_k1, diff_lambda_q2, diff_lambda_k2, diff_norm, w_out, final_norm):
    n_batch, seq, d = x_prompt.shape
    assert n_batch == 1
    n_seq, n_tok, _ = x_sample.shape
    depth = w_in.shape[0]
    t_valid = seq + META_LEN
    row_tile = 512
    t_pad = pl.cdiv(t_valid, row_tile) * row_tile
    n_s = n_seq * n_tok

    xp = jnp.concatenate([meta_tokens.astype(x_prompt.dtype), x_prompt[0]], axis=0)
    xs = x_sample.reshape(n_s, d)
    caches = [jnp.transpose(c, (0, 1, 3, 2, 4)) for c in (cache_diff_k, cache_diff_v, cache_sb_k, cache_sb_v)]

    p_rows, s_rows = [], []
    for l in range(depth):
        lam_init = 0.8 - 0.6 * math.exp(-0.3 * l)
        w_packed = _pack_w_in(w_in[l])
        w_out_bf = w_out[l].astype(BF16)
        wg_pad = jnp.zeros((COL_TILE, COL_TILE), F32).at[:GLA_RANK].set(gla_w_gate[l]).astype(BF16)
        bg = gla_b_gate[l].reshape(1, -1)
        gn = gla_norm[l].reshape(1, -1)
        dn = diff_norm[l].reshape(1, -1)
        lqk = jnp.stack([diff_lambda_q1[l], diff_lambda_k1[l], diff_lambda_q2[l], diff_lambda_k2[l]])

        pr = _inproj(xp, norm_mix[l], w_packed, t_pad=t_pad, tm=row_tile,
                     attention_copies={"d": DIFF_KEY_CHUNK, "s": SB_KEY_CHUNK})
        og, s_fin = _gla_prompt(pr["pg"], wg_pad, bg, gn, t_valid=t_valid)
        od = _diff_prompt(pr["dq"], pr["dkb"], pr["dvt"], lqk, dn, t_valid=t_valid, lam_init=lam_init)
        osb = _sb_prompt(pr["sq"], pr["skb"], pr["svt"], t_valid=t_valid)
        xp = _merge(xp, og, pr["pg"], od, pr["dg"], osb, pr["sg"], w_out_bf, tm=row_tile)
        p_rows.append(tuple(_head_major_to_cache(pr[k])[None] for k in ("dk", "dv", "sk", "sv")) + (s_fin[None],))

        sr = _inproj(xs, norm_mix[l], w_packed, t_pad=n_s, tm=n_s, attention_copies=None)
        og_s, s_new = _gla_sample(sr["pg"], state_gla, l, wg_pad, bg, gn, n_seq=n_seq, n_tok=n_tok)
        per_seq = lambda a: a.reshape(a.shape[0], n_seq, n_tok, HEAD_W).transpose(1, 0, 2, 3)
        od_s = _diff_sample(per_seq(sr["dq"]), per_seq(sr["dk"]), per_seq(sr["dv"]), caches[0], caches[1],
                            page_table, l, lqk, dn, lam_init=lam_init)
        os_s = _sb_sample(per_seq(sr["sq"]), per_seq(sr["sk"]), per_seq(sr["sv"]), caches[2], caches[3],
                          page_table, l)
        xs = _merge(xs, og_s, sr["pg"], od_s, sr["dg"], os_s, sr["sg"], w_out_bf, tm=n_s)
        to_cache = lambda a: a.reshape(a.shape[0], n_seq, n_tok, HEAD_W).transpose(1, 2, 0, 3)
        s_rows.append(tuple(to_cache(sr[k]) for k in ("dk", "dv", "sk", "sv")) + (s_new,))

    y_prompt = _final_norm(xp[META_LEN:], final_norm, tm=row_tile)[None]
    y_sample = _final_norm(xs, final_norm, tm=n_s).reshape(n_seq, n_tok, d)
    stack = lambda rows, i: jnp.stack([r[i] for r in rows])
    return (y_prompt, y_sample,
            stack(p_rows, 0), stack(p_rows, 1), stack(p_rows, 2), stack(p_rows, 3), stack(p_rows, 4),
            stack(s_rows, 0), stack(s_rows, 1), stack(s_rows, 2), stack(s_rows, 3), stack(s_rows, 4))
```

```python
import functools
import math

import jax
import jax.numpy as jnp
from jax import lax
from jax.experimental import pallas as pl
from jax.experimental.pallas import tpu as pltpu

F32 = jnp.float32
BF16 = jnp.bfloat16

META_LEN = 16
GLA_HEADS = 4
GLA_DK = 64
GLA_DV = 128
GLA_RANK = 16
GLA_TAU = 16.0
DIFF_HEADS = 6
DIFF_DH = 64
SB_HEADS = 6
SB_DH = 128
HEAD_W = 128
EPS = 1e-6
NEG = -1e30
EXP_UNDERFLOW = -110.0

GLA_WIDTH = GLA_HEADS * GLA_DV
ATT_WIDTH = DIFF_HEADS * HEAD_W
IN_WIDTHS = (256, 256, 512, 16, 512, 768, 768, 768, 768, 768, 768, 768, 768)

COL_TILE = ATT_WIDTH
HEADS_PER_TILE = COL_TILE // HEAD_W
RANK_PAD = 256
PG_WIDTH = 3 * COL_TILE
PG_Q, PG_K, PG_V, PG_G, PG_R = 0, 256, 512, 1024, 1536
PG_USED = PG_R + RANK_PAD
SEGMENTS_A = (("pg", 3), ("dq", 1), ("dk", 1), ("dv", 1), ("dg", 1))
SEGMENTS_B = (("sq", 1), ("sk", 1), ("sv", 1), ("sg", 1))
N_COL_TILES = sum(n for _, n in SEGMENTS_A + SEGMENTS_B)

VMEM_LIMIT = 56 * 1024 * 1024
DIFF_KEY_CHUNK = 512
SB_KEY_CHUNK = 256
PAGES_PER_STEP = 8
MAX_REST_PAGES_PER_STEP = 14


def _cparams(sem):
    return pltpu.CompilerParams(dimension_semantics=sem, vmem_limit_bytes=VMEM_LIMIT)


def _softplus(z):
    return jnp.maximum(z, 0.0) + jnp.log1p(jnp.exp(-jnp.abs(z)))


def _log_sigmoid(z):
    return -_softplus(-z)


def _silu(g):
    return g / (1.0 + jnp.exp(-g))


def _nt_dot(a, b):
    return lax.dot_general(a, b, (((1,), (1,)), ((), ())), preferred_element_type=F32)


def _dot(a, b):
    return jnp.dot(a, b, preferred_element_type=F32)


def _rms(o, g):
    return o * lax.rsqrt(jnp.mean(o * o, axis=-1, keepdims=True) + EPS) * g


def _pack_w_in(w):
    offs = [0]
    for wd in IN_WIDTHS:
        offs.append(offs[-1] + wd)
    c = [w[:, offs[i]:offs[i + 1]] for i in range(len(IN_WIDTHS))]
    gq, gk, gv, gr, gg = c[:5]
    pad = jnp.zeros((w.shape[0], PG_WIDTH - PG_R - GLA_RANK), w.dtype)
    packed = jnp.concatenate([gq, gk, gv, gg, gr, pad] + c[5:], axis=1).astype(BF16)
    return packed.reshape(w.shape[0], N_COL_TILES, COL_TILE).transpose(1, 0, 2)


def _inproj_kernel(x_ref, g_ref, w_ref, *refs, t_valid, tm, outs):
    out_refs = refs[:len(outs)]
    h_ref = refs[len(outs)]
    i = pl.program_id(0)
    j = pl.program_id(1)

    @pl.when(j == 0)
    def _():
        x = x_ref[...]
        y = x * lax.rsqrt(jnp.mean(x * x, axis=-1, keepdims=True) + EPS) * g_ref[...]
        rows = i * tm + lax.broadcasted_iota(jnp.int32, (tm, 1), 0)
        h_ref[...] = jnp.where(rows < t_valid, y, 0.0).astype(BF16)

    segments = sorted({(start, n) for start, n, _ in outs})
    for start, n in segments:
        @pl.when((j >= start) & (j < start + n))
        def _(start=start, n=n):
            acc = _dot(h_ref[...], w_ref[0])
            for o_ref, (o_start, _, kind) in zip(out_refs, outs):
                if o_start != start:
                    continue
                if kind == "rows":
                    o_ref[...] = acc.astype(o_ref.dtype)
                    continue
                for hh in range(HEADS_PER_TILE):
                    a = acc[:, HEAD_W * hh:HEAD_W * (hh + 1)]
                    if kind == "heads":
                        o_ref[hh] = a.astype(o_ref.dtype)
                    else:
                        at = a.T
                        tw = o_ref.shape[-1]
                        for c in range(tm // tw):
                            o_ref[hh, c] = at[:, c * tw:(c + 1) * tw].astype(o_ref.dtype)


def _inproj(x, g, w_tiles, *, t_pad, tm, segments, tile0, value_chunk):
    t_valid, d = x.shape
    assert t_pad % tm == 0
    spec = []
    start = 0
    for name, n in segments:
        if name.endswith("g"):
            spec.append((name, start, n, "rows", F32, t_pad, 0))
        elif name.endswith("q"):
            spec.append((name, start, n, "heads", BF16, t_pad, 0))
        else:
            spec.append((name, start, n, "heads", F32, t_valid, 0))
            if value_chunk is not None and name.endswith("k"):
                spec.append((name + "b", start, n, "heads", BF16, t_pad, 0))
            if value_chunk is not None and name.endswith("v"):
                assert tm % value_chunk == 0
                spec.append((name + "t", start, n, "heads_t", BF16, t_pad, value_chunk))
        start += n
    n_tiles = start
    out_shapes, out_specs, outs = [], [], []
    for key, first, n, kind, dt, rows, tw in spec:
        outs.append((first, n, kind))
        if kind == "heads":
            assert n == 1
            out_shapes.append(jax.ShapeDtypeStruct((HEADS_PER_TILE, rows, HEAD_W), dt))
            out_specs.append(pl.BlockSpec((HEADS_PER_TILE, tm, HEAD_W), lambda i, j: (0, i, 0)))
        elif kind == "heads_t":
            assert n == 1
            out_shapes.append(jax.ShapeDtypeStruct((HEADS_PER_TILE, rows // tw, HEAD_W, tw), dt))
            out_specs.append(pl.BlockSpec((HEADS_PER_TILE, tm // tw, HEAD_W, tw), lambda i, j: (0, i, 0, 0)))
        else:
            out_shapes.append(jax.ShapeDtypeStruct((rows, n * COL_TILE), dt))
            out_specs.append(pl.BlockSpec(
                (tm, COL_TILE), lambda i, j, first=first, n=n: (i, jnp.clip(j - first, 0, n - 1))))
    res = pl.pallas_call(
        functools.partial(_inproj_kernel, t_valid=t_valid, tm=tm, outs=tuple(outs)),
        out_shape=out_shapes,
        grid=(t_pad // tm, n_tiles),
        in_specs=[pl.BlockSpec((tm, d), lambda i, j: (i, 0)),
                  pl.BlockSpec((1, d), lambda i, j: (0, 0)),
                  pl.BlockSpec((1, d, COL_TILE), lambda i, j: (tile0 + j, 0, 0))],
        out_specs=out_specs,
        scratch_shapes=[pltpu.VMEM((tm, d), BF16)],
        compiler_params=_cparams(("arbitrary", "arbitrary")),
        name="inproj",
    )(x, g.reshape(1, d), w_tiles)
    return {key: r for (key, *_), r in zip(spec, res)}


def _inproj_all(x, g, w_tiles, *, t_pad, tm, value_chunks):
    n_a = sum(n for _, n in SEGMENTS_A)
    vc = value_chunks or {"d": None, "s": None}
    out = _inproj(x, g, w_tiles, t_pad=t_pad, tm=tm, segments=SEGMENTS_A, tile0=0, value_chunk=vc["d"])
    out.update(_inproj(x, g, w_tiles, t_pad=t_pad, tm=tm, segments=SEGMENTS_B, tile0=n_a, value_chunk=vc["s"]))
    return out


def _cumsum_rows(lg, n):
    ri = lax.broadcasted_iota(jnp.int32, (n, n), 0)
    ci = lax.broadcasted_iota(jnp.int32, (n, n), 1)
    tri = jnp.where(ri >= ci, 1.0, 0.0).astype(BF16)
    hi = lg.astype(BF16)
    lo = (lg - hi.astype(F32)).astype(BF16)
    return _dot(tri, hi) + _dot(tri, lo)


def _gla_prompt_kernel(pg_ref, wg_ref, bg_ref, gn_ref, og_ref, sfin_ref, s_ref, *, t_valid, chunk, sub):
    step = pl.program_id(0)

    @pl.when(step == 0)
    def _():
        s_ref[...] = jnp.zeros_like(s_ref)

    rows = step * chunk + lax.broadcasted_iota(jnp.int32, (chunk, 1), 0)
    x = _dot(pg_ref[:, PG_R:PG_R + RANK_PAD].astype(BF16), wg_ref[...]) + bg_ref[...]
    lg = jnp.where(rows < t_valid, _log_sigmoid(x) * (1.0 / GLA_TAU), 0.0)
    b = _cumsum_rows(lg, chunk)

    lane = lax.broadcasted_iota(jnp.int32, (1, HEAD_W), 1)
    head_mask = (lane < GLA_DK, lane >= GLA_DK)
    n_sub = chunk // sub
    ri = lax.broadcasted_iota(jnp.int32, (sub, chunk), 0)
    ci = lax.broadcasted_iota(jnp.int32, (sub, chunk), 1)
    krow = lax.broadcasted_iota(jnp.int32, (chunk, 1), 0)

    pair_work = []
    for p in range(2):
        qp = pg_ref[:, PG_Q + HEAD_W * p:PG_Q + HEAD_W * (p + 1)] * (GLA_DK ** -0.5)
        kp = pg_ref[:, PG_K + HEAD_W * p:PG_K + HEAD_W * (p + 1)]
        bp = b[:, HEAD_W * p:HEAD_W * (p + 1)]
        s_old = s_ref[p]
        s_bf = s_old.astype(BF16)
        qe = qp * jnp.exp(bp)
        b_end = bp[chunk - 1:chunk]
        kend_t = (kp * jnp.exp(b_end - bp)).T
        dec_t = jnp.broadcast_to(jnp.exp(b_end), (HEAD_W, HEAD_W)).T
        q_blk, k_blk = [], []
        for blk in range(n_sub):
            lo, hi = blk * sub, (blk + 1) * sub
            r = jnp.zeros((1, HEAD_W), F32) if blk == 0 else bp[lo - 1:lo]
            q_blk.append(qp[lo:hi] * jnp.exp(bp[lo:hi] - r))
            k_blk.append((kp * jnp.exp(jnp.where(krow < hi, r - bp, 0.0))).astype(BF16))
        upd, o_inter, att_raw, vbs = [], [], [], []
        for hh in range(2):
            h = 2 * p + hh
            vb = pg_ref[:, PG_V + GLA_DV * h:PG_V + GLA_DV * (h + 1)].astype(BF16)
            vbs.append(vb)
            o_inter.append(_dot(jnp.where(head_mask[hh], qe, 0.0).astype(BF16), s_bf))
            upd.append(_dot(kend_t[GLA_DK * hh:GLA_DK * (hh + 1)].astype(BF16), vb))
            att_raw.append([_nt_dot(jnp.where(head_mask[hh], q_blk[blk], 0.0).astype(BF16), k_blk[blk])
                            for blk in range(n_sub)])
        s_ref[p] = dec_t * s_old + jnp.concatenate(upd, axis=0)
        pair_work.append((o_inter, att_raw, vbs))

    for p in range(2):
        o_inter, att_raw, vbs = pair_work[p]
        for hh in range(2):
            h = 2 * p + hh
            att = [jnp.where(ci <= ri + blk * sub, att_raw[hh][blk], 0.0).astype(BF16) for blk in range(n_sub)]
            o = o_inter[hh] + jnp.concatenate([_dot(a, vbs[hh]) for a in att], axis=0)
            og_ref[:, GLA_DV * h:GLA_DV * (h + 1)] = _rms(o, gn_ref[...])

    @pl.when(step == pl.num_programs(0) - 1)
    def _():
        sfin_ref[...] = s_ref[...]


def _gla_prompt(pg, wg_pad, bg, gn, *, t_valid, chunk=128, sub=16):
    t_pad = pg.shape[0]
    n_steps = pl.cdiv(t_valid, chunk)
    assert n_steps * chunk <= t_pad
    og, sfin = pl.pallas_call(
        functools.partial(_gla_prompt_kernel, t_valid=t_valid, chunk=chunk, sub=sub),
        out_shape=[jax.ShapeDtypeStruct((n_steps * chunk, GLA_WIDTH), F32),
                   jax.ShapeDtypeStruct((2, HEAD_W, GLA_DV), F32)],
        grid=(n_steps,),
        in_specs=[pl.BlockSpec((chunk, PG_USED), lambda i: (i, 0)),
                  pl.BlockSpec((RANK_PAD, GLA_HEADS * GLA_DK), lambda i: (0, 0)),
                  pl.BlockSpec((1, GLA_HEADS * GLA_DK), lambda i: (0, 0)),
                  pl.BlockSpec((1, GLA_DV), lambda i: (0, 0))],
        out_specs=[pl.BlockSpec((chunk, GLA_WIDTH), lambda i: (i, 0)),
                   pl.BlockSpec((2, HEAD_W, GLA_DV), lambda i: (0, 0, 0))],
        scratch_shapes=[pltpu.VMEM((2, HEAD_W, GLA_DV), F32)],
        compiler_params=_cparams(("arbitrary",)),
        name="gla_prompt",
    )(pg, wg_pad, bg, gn)
    return og, sfin.reshape(GLA_HEADS, GLA_DK, GLA_DV)


def _gla_sample_kernel(pg_ref, st_ref, wg_ref, bg_ref, gn_ref, og_ref, snew_ref, *, n_tok):
    pad = HEAD_W - n_tok
    zpad = jnp.zeros((pad, HEAD_W), F32)
    gr = jnp.concatenate([pg_ref[0, :, PG_R:PG_R + RANK_PAD], jnp.zeros((pad, RANK_PAD), F32)], axis=0)
    x = _dot(gr.astype(BF16), wg_ref[...]) + bg_ref[...]
    rows = lax.broadcasted_iota(jnp.int32, (HEAD_W, 1), 0)
    lg = jnp.where(rows < n_tok, _log_sigmoid(x) * (1.0 / GLA_TAU), 0.0)
    b = _cumsum_rows(lg, HEAD_W)

    lane = lax.broadcasted_iota(jnp.int32, (1, HEAD_W), 1)
    head_mask = (lane < GLA_DK, lane >= GLA_DK)
    ri = lax.broadcasted_iota(jnp.int32, (HEAD_W, HEAD_W), 0)
    ci = lax.broadcasted_iota(jnp.int32, (HEAD_W, HEAD_W), 1)
    og = []
    for p in range(2):
        qp = jnp.concatenate([pg_ref[0, :,PG_Q + HEAD_W * p:PG_Q + HEAD_W * (p + 1)], zpad], axis=0) * (GLA_DK ** -0.5)
        kp = jnp.concatenate([pg_ref[0, :,PG_K + HEAD_W * p:PG_K + HEAD_W * (p + 1)], zpad], axis=0)
        bp = b[:, HEAD_W * p:HEAD_W * (p + 1)]
        s_old = jnp.concatenate([st_ref[0, 0, 2 * p], st_ref[0, 0, 2 * p + 1]], axis=0)
        s_bf = s_old.astype(BF16)
        qe = qp * jnp.exp(bp)
        kinv = (kp * jnp.exp(-bp)).astype(BF16)
        b_end = bp[HEAD_W - 1:HEAD_W]
        kend_t = (kp * jnp.exp(b_end - bp)).T
        dec_t = jnp.broadcast_to(jnp.exp(b_end), (HEAD_W, HEAD_W)).T
        upd = []
        for hh in range(2):
            h = 2 * p + hh
            vb = jnp.concatenate([pg_ref[0, :,PG_V + GLA_DV * h:PG_V + GLA_DV * (h + 1)], zpad], axis=0).astype(BF16)
            qm = jnp.where(head_mask[hh], qe, 0.0).astype(BF16)
            att = jnp.where(ci <= ri, _nt_dot(qm, kinv), 0.0)
            o = _dot(qm, s_bf) + _dot(att.astype(BF16), vb)
            og.append(_rms(o, gn_ref[...])[:n_tok])
            upd.append(_dot(kend_t[GLA_DK * hh:GLA_DK * (hh + 1)].astype(BF16), vb))
        s_new = dec_t * s_old + jnp.concatenate(upd, axis=0)
        snew_ref[0, 2 * p] = s_new[:GLA_DK]
        snew_ref[0, 2 * p + 1] = s_new[GLA_DK:]
    og_ref[0] = jnp.concatenate(og, axis=1)


def _gla_sample(pg, state, layer, wg_pad, bg, gn, *, n_seq, n_tok):
    pg3 = pg.reshape(n_seq, n_tok, PG_WIDTH)
    og, snew = pl.pallas_call(
        functools.partial(_gla_sample_kernel, n_tok=n_tok),
        out_shape=[jax.ShapeDtypeStruct((n_seq, n_tok, GLA_WIDTH), F32),
                   jax.ShapeDtypeStruct((n_seq, GLA_HEADS, GLA_DK, GLA_DV), F32)],
        grid=(n_seq,),
        in_specs=[pl.BlockSpec((1, n_tok, PG_USED), lambda i: (i, 0, 0)),
                  pl.BlockSpec((1, 1, GLA_HEADS, GLA_DK, GLA_DV), lambda i: (layer, i, 0, 0, 0)),
                  pl.BlockSpec((RANK_PAD, GLA_HEADS * GLA_DK), lambda i: (0, 0)),
                  pl.BlockSpec((1, GLA_HEADS * GLA_DK), lambda i: (0, 0)),
                  pl.BlockSpec((1, GLA_DV), lambda i: (0, 0))],
        out_specs=[pl.BlockSpec((1, n_tok, GLA_WIDTH), lambda i: (i, 0, 0)),
                   pl.BlockSpec((1, GLA_HEADS, GLA_DK, GLA_DV), lambda i: (i, 0, 0, 0))],
        compiler_params=_cparams(("arbitrary",)),
        name="gla_sample",
    )(pg3, state, wg_pad, bg, gn)
    return og.reshape(n_seq * n_tok, GLA_WIDTH), snew


def _diff_lambda(lqk_ref, lam_init):
    s1 = jnp.sum(lqk_ref[0:1, :] * lqk_ref[1:2, :], axis=-1, keepdims=True)
    s2 = jnp.sum(lqk_ref[2:3, :] * lqk_ref[3:4, :], axis=-1, keepdims=True)
    return jnp.exp(s1) - jnp.exp(s2) + lam_init


def _split_maps(q):
    lane = lax.broadcasted_iota(jnp.int32, q.shape, 1)
    zero = jnp.zeros_like(q)
    return jnp.concatenate([jnp.where(lane < DIFF_DH, q, zero), jnp.where(lane >= DIFF_DH, q, zero)], axis=0)


def _diff_prompt_kernel(lqk_ref, dn_ref, q_ref, k_ref, vt_ref, o_ref, m_ref, l_ref, acc_ref, s_ref, *, bq, bk,
                        lam_init):
    i = pl.program_id(1)
    qq = _split_maps(q_ref[0]) * (DIFF_DH ** -0.5)
    lane = lax.broadcasted_iota(jnp.int32, (1, 2 * bq), 1)
    qpos = i * bq + jnp.where(lane < bq, lane, lane - bq)
    m_ref[...] = jnp.full_like(m_ref, NEG)
    l_ref[...] = jnp.zeros_like(l_ref)
    acc_ref[...] = jnp.zeros_like(acc_ref)

    n_full = (i * bq + 1) // bk
    n_need = (i * bq + bq + bk - 1) // bk

    def scores(j):
        off = pl.multiple_of(j * bk, bk)
        return _nt_dot(k_ref[0, pl.ds(off, bk), :], qq)

    s_ref[...] = scores(0)

    def chunk(j, masked):
        s = s_ref[...]
        s_next = scores(jnp.minimum(j + 1, n_need - 1))
        if masked:
            kpos = j * bk + lax.broadcasted_iota(jnp.int32, (bk, 1), 0)
            s = jnp.where(kpos <= qpos, s, NEG)
        m_old = m_ref[...]
        m_new = jnp.maximum(m_old, jnp.max(s, axis=0, keepdims=True))
        alpha = jnp.exp(m_old - m_new)
        p = jnp.exp(s - m_new)
        l_ref[...] = alpha * l_ref[...] + jnp.sum(p, axis=0, keepdims=True)
        acc_ref[...] = alpha * acc_ref[...] + _dot(vt_ref[0, j], p.astype(BF16))
        m_ref[...] = m_new
        s_ref[...] = s_next

    def full_body(j, c):
        chunk(j, False)
        return c

    def diag_body(j, c):
        chunk(j, True)
        return c

    def pair_body(t, c):
        chunk(2 * t, False)
        chunk(2 * t + 1, False)
        return c

    n_pairs = n_full // 2
    lax.fori_loop(0, n_pairs, pair_body, 0)
    lax.fori_loop(2 * n_pairs, n_full, full_body, 0)
    lax.fori_loop(n_full, n_need, diag_body, 0)

    lam = _diff_lambda(lqk_ref, lam_init)
    on = acc_ref[...] / l_ref[...]
    o = (on[:, :bq] - lam * on[:, bq:]).T
    o_ref[...] = _rms(o, dn_ref[...]) * (1.0 - lam_init)


def _diff_prompt(q, k, vt, lqk, dn, *, t_valid, lam_init, bq=256):
    n_h, t_pad, _ = q.shape
    bk = vt.shape[-1]
    nq = pl.cdiv(t_valid, bq)
    assert pl.cdiv(nq * bq, bk) * bk <= t_pad
    return pl.pallas_call(
        functools.partial(_diff_prompt_kernel, bq=bq, bk=bk, lam_init=lam_init),
        out_shape=jax.ShapeDtypeStruct((nq * bq, n_h * HEAD_W), F32),
        grid=(n_h, nq),
        in_specs=[pl.BlockSpec((4, DIFF_DH), lambda h, i: (0, 0)),
                  pl.BlockSpec((1, HEAD_W), lambda h, i: (0, 0)),
                  pl.BlockSpec((1, bq, HEAD_W), lambda h, i: (h, i, 0)),
                  pl.BlockSpec((1, t_pad, HEAD_W), lambda h, i: (h, 0, 0)),
                  pl.BlockSpec((1, t_pad // bk, HEAD_W, bk), lambda h, i: (h, 0, 0, 0))],
        out_specs=pl.BlockSpec((bq, HEAD_W), lambda h, i: (i, h)),
        scratch_shapes=[pltpu.VMEM((1, 2 * bq), F32), pltpu.VMEM((1, 2 * bq), F32),
                        pltpu.VMEM((HEAD_W, 2 * bq), F32), pltpu.VMEM((bk, 2 * bq), F32)],
        compiler_params=_cparams(("arbitrary", "arbitrary")),
        name="diff_prompt",
    )(lqk, dn, q, k, vt)


def _bf16_round(x):
    return x.astype(BF16).astype(F32)


def _diff_sample_kernel(pt_ref, lqk_ref, dn_ref, q_ref, kn_ref, vn_ref, *refs, n_tok, pages_per_step, lam_init):
    g_pages = pages_per_step
    k_refs = refs[:g_pages]
    v_refs = refs[g_pages:2 * g_pages]
    o_ref = refs[2 * g_pages]
    qq_ref, m_ref, l_ref, acc_ref = refs[2 * g_pages + 1:]
    n_h = q_ref.shape[1]
    n2 = 2 * n_tok
    p = pl.program_id(1)

    @pl.when(p == 0)
    def _():
        qrow = lax.broadcasted_iota(jnp.int32, (n2, 1), 0) % n_tok
        for h in range(n_h):
            qq = _split_maps(q_ref[0, h]) * (DIFF_DH ** -0.5)
            qq_ref[h] = qq.astype(F32)
            qf = qq.astype(F32)
            kn = _bf16_round(kn_ref[0, h])
            vn = _bf16_round(vn_ref[0, h])
            s = [jnp.where(qrow >= t, jnp.sum(qf * kn[t:t + 1], axis=-1, keepdims=True), NEG) for t in range(n_tok)]
            m = s[0]
            for t in range(1, n_tok):
                m = jnp.maximum(m, s[t])
            l = jnp.zeros((n2, 1), F32)
            acc = jnp.zeros((n2, HEAD_W), F32)
            for t in range(n_tok):
                pt = jnp.exp(s[t] - m)
                l = l + pt
                acc = acc + _bf16_round(pt) * vn[t:t + 1]
            m_ref[h] = m
            l_ref[h] = l
            acc_ref[h] = acc

    page = k_refs[0].shape[3]
    scores = []
    for h in range(n_h):
        qh = qq_ref[h].astype(BF16)
        scores.append(jnp.concatenate(
            [_nt_dot(qh, k_refs[g][0, 0, h].astype(BF16)) for g in range(g_pages)], axis=1))
    probs, alphas = [], []
    for h in range(n_h):
        s = scores[h]
        m_old = m_ref[h]
        m_new = jnp.maximum(m_old, jnp.max(s, axis=-1, keepdims=True))
        alpha = jnp.exp(m_old - m_new)
        p32 = jnp.exp(s - m_new)
        l_ref[h] = alpha * l_ref[h] + jnp.sum(p32, axis=-1, keepdims=True)
        m_ref[h] = m_new
        probs.append(p32.astype(BF16))
        alphas.append(alpha)
    for h in range(n_h):
        pv = _dot(probs[h][:, :page], v_refs[0][0, 0, h].astype(BF16))
        for g in range(1, g_pages):
            pv += _dot(probs[h][:, g * page:(g + 1) * page], v_refs[g][0, 0, h].astype(BF16))
        acc_ref[h] = alphas[h] * acc_ref[h] + pv

    @pl.when(p == pl.num_programs(1) - 1)
    def _():
        lam = _diff_lambda(lqk_ref, lam_init)
        outs = []
        for h in range(n_h):
            on = acc_ref[h] / l_ref[h]
            o = on[:n_tok] - lam * on[n_tok:]
            outs.append(_rms(o, dn_ref[...]) * (1.0 - lam_init))
        o_ref[0] = jnp.concatenate(outs, axis=1)


def _page_specs(layer, n_pages, pages_per_step, n_h, page, reverse):
    specs = []
    for g in range(pages_per_step):
        def imap(b, p, pt, g=g):
            idx = p * pages_per_step + g
            if reverse:
                idx = n_pages - 1 - idx
            return (layer, pt[b * n_pages + idx], 0, 0, 0)
        specs.append(pl.BlockSpec((1, 1, n_h, page, HEAD_W), imap))
    return specs


def _diff_sample(q, kn, vn, cache_k, cache_v, page_table, layer, lqk, dn, *, lam_init, pages_per_step=PAGES_PER_STEP):
    n_seq, n_h, n_tok, _ = q.shape
    n_pages = page_table.shape[1]
    page = cache_k.shape[3]
    assert n_pages % pages_per_step == 0
    tok_spec = pl.BlockSpec((1, n_h, n_tok, HEAD_W), lambda b, p, pt: (b, 0, 0, 0))
    pspecs = _page_specs(layer, n_pages, pages_per_step, n_h, page, reverse=False)
    out = pl.pallas_call(
        functools.partial(_diff_sample_kernel, n_tok=n_tok, pages_per_step=pages_per_step, lam_init=lam_init),
        out_shape=jax.ShapeDtypeStruct((n_seq, n_tok, n_h * HEAD_W), F32),
        grid_spec=pltpu.PrefetchScalarGridSpec(
            num_scalar_prefetch=1,
            grid=(n_seq, n_pages // pages_per_step),
            in_specs=[pl.BlockSpec((4, DIFF_DH), lambda b, p, pt: (0, 0)),
                      pl.BlockSpec((1, HEAD_W), lambda b, p, pt: (0, 0)),
                      tok_spec, tok_spec, tok_spec] + pspecs + pspecs,
            out_specs=pl.BlockSpec((1, n_tok, n_h * HEAD_W), lambda b, p, pt: (b, 0, 0)),
            scratch_shapes=[pltpu.VMEM((n_h, 2 * n_tok, HEAD_W), F32), pltpu.VMEM((n_h, 2 * n_tok, 1), F32),
                            pltpu.VMEM((n_h, 2 * n_tok, 1), F32), pltpu.VMEM((n_h, 2 * n_tok, HEAD_W), F32)],
        ),
        compiler_params=_cparams(("arbitrary", "arbitrary")),
        name="diff_sample",
    )(page_table.reshape(-1), lqk, dn, q, kn, vn, *([cache_k] * pages_per_step), *([cache_v] * pages_per_step))
    return out.reshape(n_seq * n_tok, n_h * HEAD_W)


def _later_keys_matrix(n):
    ri = lax.broadcasted_iota(jnp.int32, (n, n), 0)
    ci = lax.broadcasted_iota(jnp.int32, (n, n), 1)
    return jnp.where(ri > ci, 1.0, 0.0).astype(BF16)


def _tail_sums(lk, u):
    n = lk.shape[0]
    hi = lk.astype(BF16)
    lo = (lk - hi.astype(F32)).astype(BF16)
    t = _dot(jnp.concatenate([hi, lo], axis=0), u)
    return t[:n] + t[n:]


def _sb_prompt_kernel(q_ref, k_ref, vt_ref, o_ref, c_ref, acc_ref, *, blk):
    i = pl.program_id(1)
    q = q_ref[0]
    ri = lax.broadcasted_iota(jnp.int32, (blk, blk), 0)
    ci = lax.broadcasted_iota(jnp.int32, (blk, blk), 1)
    later = jnp.where(ci > ri, 1.0, 0.0).astype(BF16)
    qpos = i * blk + lax.broadcasted_iota(jnp.int32, (1, blk), 1)
    c_ref[...] = jnp.zeros_like(c_ref)
    acc_ref[...] = jnp.zeros_like(acc_ref)

    def chunk(j, masked):
        off = pl.multiple_of(j * blk, blk)
        k = k_ref[0, pl.ds(off, blk), :]
        z = _nt_dot(k, q) * (SB_DH ** -0.5)
        sp = _softplus(z)
        lk = -sp
        lb = z - sp
        if masked:
            kpos = j * blk + lax.broadcasted_iota(jnp.int32, (blk, 1), 0)
            vis = kpos < qpos
            lk = jnp.where(vis, lk, 0.0)
        hi = lk.astype(BF16)
        lo = (lk - hi.astype(F32)).astype(BF16)
        t = _dot(later, jnp.concatenate([hi, lo], axis=1))
        c = c_ref[...]
        a = jnp.exp(lb + (t[:, :blk] + t[:, blk:]) + c)
        if masked:
            a = jnp.where(vis, a, 0.0)
        acc_ref[...] += _dot(vt_ref[0, j], a.astype(BF16))
        c_new = c + jnp.sum(lk, axis=0, keepdims=True)
        c_ref[...] = c_new
        return jnp.max(c_new)

    def cond(carry):
        t, c_max = carry
        return jnp.logical_and(t < i, c_max > EXP_UNDERFLOW)

    def body(carry):
        t, _ = carry
        return t + 1, chunk(i - 1 - t, False)

    lax.while_loop(cond, body, (jnp.int32(0), chunk(i, True)))
    o_ref[...] = acc_ref[...].T


def _sb_prompt(q, k, vt, *, t_valid):
    n_h, t_pad, _ = q.shape
    blk = vt.shape[-1]
    nq = pl.cdiv(t_valid, blk)
    assert nq * blk <= t_pad
    return pl.pallas_call(
        functools.partial(_sb_prompt_kernel, blk=blk),
        out_shape=jax.ShapeDtypeStruct((nq * blk, n_h * HEAD_W), F32),
        grid=(n_h, nq),
        in_specs=[pl.BlockSpec((1, blk, HEAD_W), lambda h, i: (h, i, 0)),
                  pl.BlockSpec((1, t_pad, HEAD_W), lambda h, i: (h, 0, 0)),
                  pl.BlockSpec((1, t_pad // blk, HEAD_W, blk), lambda h, i: (h, 0, 0, 0))],
        out_specs=pl.BlockSpec((blk, HEAD_W), lambda h, i: (i, h)),
        scratch_shapes=[pltpu.VMEM((1, blk), F32), pltpu.VMEM((HEAD_W, blk), F32)],
        compiler_params=_cparams(("arbitrary", "arbitrary")),
        name="sb_prompt",
    )(q, k, vt)


def _sb_new_tokens(q8, kn, vn, n_tok):
    rows8 = q8.shape[0]
    qrow = lax.broadcasted_iota(jnp.int32, (rows8, 1), 0)
    c = jnp.zeros((rows8, 1), F32)
    acc = jnp.zeros((rows8, HEAD_W), F32)
    for t in range(n_tok - 1, -1, -1):
        z = jnp.sum(q8 * kn[t:t + 1], axis=-1, keepdims=True) * (SB_DH ** -0.5)
        sp = _softplus(z)
        vis = qrow > t
        a = jnp.where(vis, jnp.exp(z - sp + c), 0.0)
        acc = acc + _bf16_round(a) * vn[t:t + 1]
        c = c + jnp.where(vis, -sp, 0.0)
    return c, acc


def _sb_sample_kernel(pt_ref, done_ref, q_ref, *refs, n_tok, pages_per_step, first_stage):
    g_pages = pages_per_step
    seed_a, seed_b = refs[:2]
    k_refs = refs[2:2 + g_pages]
    v_refs = refs[2 + g_pages:2 + 2 * g_pages]
    n_out = 3 if first_stage else 1
    out_refs = refs[2 + 2 * g_pages:2 + 2 * g_pages + n_out]
    c_ref, acc_ref = refs[2 + 2 * g_pages + n_out:]
    n_h = q_ref.shape[1]
    page = k_refs[0].shape[3]
    rows8 = 8
    b = pl.program_id(0)
    p = pl.program_id(1)
    q8s = [jnp.concatenate([q_ref[0, h].astype(F32), jnp.zeros((rows8 - n_tok, HEAD_W), F32)], axis=0)
           for h in range(n_h)]

    @pl.when(p == 0)
    def _():
        for h in range(n_h):
            if first_stage:
                c, acc = _sb_new_tokens(q8s[h], _bf16_round(seed_a[0, h]), _bf16_round(seed_b[0, h]), n_tok)
            else:
                c, acc = seed_a[0, h], seed_b[0, h]
            c_ref[h] = c
            acc_ref[h] = acc

    def visit_pages():
        _sb_visit_pages(q8s, k_refs, v_refs, c_ref, acc_ref, page)

    if first_stage:
        visit_pages()
    else:
        pl.when(done_ref[b] == 0)(visit_pages)

    @pl.when(p == pl.num_programs(1) - 1)
    def _():
        if first_stage:
            c_out, acc_out, done_out = out_refs
            c_max = jnp.full((1, 1), NEG, F32)
            for h in range(n_h):
                c_out[0, h] = c_ref[h]
                acc_out[0, h] = acc_ref[h]
                c_max = jnp.maximum(c_max, jnp.max(c_ref[h][:n_tok], axis=0, keepdims=True))
            done_out[0] = jnp.where(c_max < EXP_UNDERFLOW, 1, 0).astype(jnp.int32)
        else:
            out_refs[0][0] = jnp.concatenate([acc_ref[h][:n_tok] for h in range(n_h)], axis=1)


def _sb_visit_pages(q8s, k_refs, v_refs, c_ref, acc_ref, page):
    g_pages = len(k_refs)
    n_h = len(q8s)
    rows8 = q8s[0].shape[0]
    u = _later_keys_matrix(page)
    zs = []
    for h in range(n_h):
        qh = q8s[h].astype(BF16)
        zs.append(jnp.concatenate(
            [_nt_dot(qh, k_refs[g][0, 0, h].astype(BF16)) for g in range(g_pages)], axis=0) * (SB_DH ** -0.5))
    lbs, tails, psums = [], [], []
    for h in range(n_h):
        sp = _softplus(zs[h])
        lbs.append(zs[h] - sp)
        tails.append(_tail_sums(-sp, u))
        psums.append(jnp.sum(-sp, axis=-1, keepdims=True))
    for h in range(n_h):
        c = c_ref[h]
        cs = []
        for g in range(g_pages):
            cs.append(c)
            c = c + psums[h][rows8 * g:rows8 * (g + 1)]
        c_ref[h] = c
        a = jnp.exp(lbs[h] + tails[h] + jnp.concatenate(cs, axis=0))
        pv = _dot(a[:rows8].astype(BF16), v_refs[0][0, 0, h].astype(BF16))
        for g in range(1, g_pages):
            pv += _dot(a[rows8 * g:rows8 * (g + 1)].astype(BF16), v_refs[g][0, 0, h].astype(BF16))
        acc_ref[h] += pv


def _sb_sample(q, kn, vn, cache_k, cache_v, page_table, layer, *, pages_per_step=PAGES_PER_STEP):
    n_seq, n_h, n_tok, _ = q.shape
    n_pages = page_table.shape[1]
    page = cache_k.shape[3]
    g_pages = pages_per_step
    assert n_pages % g_pages == 0 and n_pages > g_pages and n_tok <= 8
    pt = page_table.reshape(-1)
    tok_spec = pl.BlockSpec((1, n_h, n_tok, HEAD_W), lambda b, p, pt, done: (b, 0, 0, 0))
    c_spec = pl.BlockSpec((1, n_h, 8, 1), lambda b, p, pt, done: (b, 0, 0, 0))
    acc_spec = pl.BlockSpec((1, n_h, 8, HEAD_W), lambda b, p, pt, done: (b, 0, 0, 0))
    scratch = [pltpu.VMEM((n_h, 8, 1), F32), pltpu.VMEM((n_h, 8, HEAD_W), F32)]

    def page_specs(first_idx, skip_done, per_step):
        specs = []
        for g in range(per_step):
            def imap(b, p, pt, done, g=g):
                phys = pt[b * n_pages + (n_pages - 1 - (first_idx + p * per_step + g))]
                if skip_done:
                    phys = jnp.where(done[b] == 0, phys, 0)
                return (layer, phys, 0, 0, 0)
            specs.append(pl.BlockSpec((1, 1, n_h, page, HEAD_W), imap))
        return specs

    specs1 = page_specs(0, False, g_pages)
    c1, acc1, done = pl.pallas_call(
        functools.partial(_sb_sample_kernel, n_tok=n_tok, pages_per_step=g_pages, first_stage=True),
        out_shape=[jax.ShapeDtypeStruct((n_seq, n_h, 8, 1), F32),
                   jax.ShapeDtypeStruct((n_seq, n_h, 8, HEAD_W), F32),
                   jax.ShapeDtypeStruct((n_seq, 1, 1), jnp.int32)],
        grid_spec=pltpu.PrefetchScalarGridSpec(
            num_scalar_prefetch=2,
            grid=(n_seq, 1),
            in_specs=[tok_spec, tok_spec, tok_spec] + specs1 + specs1,
            out_specs=[c_spec, acc_spec, pl.BlockSpec((1, 1, 1), lambda b, p, pt, done: (b, 0, 0))],
            scratch_shapes=scratch,
        ),
        compiler_params=_cparams(("arbitrary", "arbitrary")),
        name="sb_sample_newest",
    )(pt, jnp.zeros((n_seq,), jnp.int32), q, kn, vn, *([cache_k] * g_pages), *([cache_v] * g_pages))

    n_rest = n_pages - g_pages
    g_rest = max(g for g in range(1, MAX_REST_PAGES_PER_STEP + 1) if n_rest % g == 0)
    specs2 = page_specs(g_pages, True, g_rest)
    out = pl.pallas_call(
        functools.partial(_sb_sample_kernel, n_tok=n_tok, pages_per_step=g_rest, first_stage=False),
        out_shape=jax.ShapeDtypeStruct((n_seq, n_tok, n_h * HEAD_W), F32),
        grid_spec=pltpu.PrefetchScalarGridSpec(
            num_scalar_prefetch=2,
            grid=(n_seq, n_rest // g_rest),
            in_specs=[tok_spec, c_spec, acc_spec] + specs2 + specs2,
            out_specs=pl.BlockSpec((1, n_tok, n_h * HEAD_W), lambda b, p, pt, done: (b, 0, 0)),
            scratch_shapes=scratch,
        ),
        compiler_params=_cparams(("arbitrary", "arbitrary")),
        name="sb_sample_rest",
    )(pt, done.reshape(-1), q, c1, acc1, *([cache_k] * g_rest), *([cache_v] * g_rest))
    return out.reshape(n_seq * n_tok, n_h * HEAD_W)


def _merge_kernel(x_ref, og_ref, gg_ref, od_ref, dg_ref, os_ref, sg_ref, w_ref, fg_ref, o_ref, *, final_norm):
    ag = (og_ref[...] * _silu(gg_ref[...])).astype(BF16)
    ad = (od_ref[...] * _silu(dg_ref[...])).astype(BF16)
    asb = (os_ref[...] * _silu(sg_ref[...])).astype(BF16)
    y = _dot(ag, w_ref[0:GLA_WIDTH, :])
    y += _dot(ad, w_ref[GLA_WIDTH:GLA_WIDTH + ATT_WIDTH, :])
    y += _dot(asb, w_ref[GLA_WIDTH + ATT_WIDTH:, :])
    out = x_ref[...] + y
    o_ref[...] = _rms(out, fg_ref[...]) if final_norm else out


def _merge(x, og, pg, od, dg, osb, sg, w_out, final_g, *, tm, final_norm):
    t_valid, d = x.shape
    assert PG_G % GLA_WIDTH == 0
    return pl.pallas_call(
        functools.partial(_merge_kernel, final_norm=final_norm),
        out_shape=jax.ShapeDtypeStruct((t_valid, d), F32),
        grid=(pl.cdiv(t_valid, tm),),
        in_specs=[pl.BlockSpec((tm, d), lambda i: (i, 0)),
                  pl.BlockSpec((tm, GLA_WIDTH), lambda i: (i, 0)),
                  pl.BlockSpec((tm, GLA_WIDTH), lambda i: (i, PG_G // GLA_WIDTH)),
                  pl.BlockSpec((tm, ATT_WIDTH), lambda i: (i, 0)),
                  pl.BlockSpec((tm, ATT_WIDTH), lambda i: (i, 0)),
                  pl.BlockSpec((tm, ATT_WIDTH), lambda i: (i, 0)),
                  pl.BlockSpec((tm, ATT_WIDTH), lambda i: (i, 0)),
                  pl.BlockSpec(w_out.shape, lambda i: (0, 0)),
                  pl.BlockSpec((1, d), lambda i: (0, 0))],
        out_specs=pl.BlockSpec((tm, d), lambda i: (i, 0)),
        compiler_params=_cparams(("arbitrary",)),
        name="merge",
    )(x, og, pg, od, dg, osb, sg, w_out, final_g.reshape(1, d))


def _head_major_to_cache(a):
    return jnp.transpose(a, (1, 0, 2))


def kernel(x_prompt, x_sample, cache_diff_k, cache_diff_v, cache_sb_k, cache_sb_v, state_gla, page_table, meta_tokens, norm_mix, w_in, gla_w_gate, gla_b_gate, gla_norm, diff_lambda_q1, diff_lambda_k1, diff_lambda_q2, diff_lambda_k2, diff_norm, w_out, final_norm):
    n_batch, seq, d = x_prompt.shape
    assert n_batch == 1
    n_seq, n_tok, _ = x_sample.shape
    depth = w_in.shape[0]
    t_valid = seq + META_LEN
    row_tile = 512
    t_pad = pl.cdiv(t_valid, row_tile) * row_tile
    n_s = n_seq * n_tok

    xp = jnp.concatenate([meta_tokens.astype(x_prompt.dtype), x_prompt[0]], axis=0)
    xs = x_sample.reshape(n_s, d)
    caches = [jnp.transpose(c, (0, 1, 3, 2, 4)) for c in (cache_diff_k, cache_diff_v, cache_sb_k, cache_sb_v)]

    p_rows, s_rows = [], []
    for l in range(depth):
        lam_init = 0.8 - 0.6 * math.exp(-0.3 * l)
        last = l == depth - 1
        w_packed = _pack_w_in(w_in[l])
        w_out_bf = w_out[l].astype(BF16)
        wg_pad = jnp.zeros((RANK_PAD, GLA_HEADS * GLA_DK), F32).at[:GLA_RANK].set(gla_w_gate[l]).astype(BF16)
        bg = gla_b_gate[l].reshape(1, -1)
        gn = gla_norm[l].reshape(1, -1)
        dn = diff_norm[l].reshape(1, -1)
        lqk = jnp.stack([diff_lambda_q1[l], diff_lambda_k1[l], diff_lambda_q2[l], diff_lambda_k2[l]])

        pr = _inproj_all(xp, norm_mix[l], w_packed, t_pad=t_pad, tm=row_tile,
                         value_chunks={"d": DIFF_KEY_CHUNK, "s": SB_KEY_CHUNK})
        og, s_fin = _gla_prompt(pr["pg"], wg_pad, bg, gn, t_valid=t_valid)
        od = _diff_prompt(pr["dq"], pr["dkb"], pr["dvt"], lqk, dn, t_valid=t_valid, lam_init=lam_init)
        osb = _sb_prompt(pr["sq"], pr["skb"], pr["svt"], t_valid=t_valid)
        xp = _merge(xp, og, pr["pg"], od, pr["dg"], osb, pr["sg"], w_out_bf, final_norm, tm=row_tile,
                    final_norm=last)
        p_rows.append(tuple(_head_major_to_cache(pr[k])[None] for k in ("dk", "dv", "sk", "sv")) + (s_fin[None],))

        sr = _inproj_all(xs, norm_mix[l], w_packed, t_pad=n_s, tm=n_s, value_chunks=None)
        og_s, s_new = _gla_sample(sr["pg"], state_gla, l, wg_pad, bg, gn, n_seq=n_seq, n_tok=n_tok)
        per_seq = lambda a: a.reshape(a.shape[0], n_seq, n_tok, HEAD_W).transpose(1, 0, 2, 3)
        od_s = _diff_sample(per_seq(sr["dq"]), per_seq(sr["dk"]), per_seq(sr["dv"]), caches[0], caches[1],
                            page_table, l, lqk, dn, lam_init=lam_init)
        os_s = _sb_sample(per_seq(sr["sq"]), per_seq(sr["sk"]), per_seq(sr["sv"]), caches[2], caches[3],
                          page_table, l)
        xs = _merge(xs, og_s, sr["pg"], od_s, sr["dg"], os_s, sr["sg"], w_out_bf, final_norm, tm=n_s,
                    final_norm=last)
        to_cache = lambda a: a.reshape(a.shape[0], n_seq, n_tok, HEAD_W).transpose(1, 2, 0, 3)
        s_rows.append(tuple(to_cache(sr[k]) for k in ("dk", "dv", "sk", "sv")) + (s_new,))

    y_prompt = xp[META_LEN:][None]
    y_sample = xs.reshape(n_seq, n_tok, d)
    stack = lambda rows, i: jnp.stack([r[i] for r in rows])
    return (y_prompt, y_sample,
            stack(p_rows, 0), stack(p_rows, 1), stack(p_rows, 2), stack(p_rows, 3), stack(p_rows, 4),
            stack(s_rows, 0), stack(s_rows, 1), stack(s_rows, 2), stack(s_rows, 3), stack(s_rows, 4))
```

```python
import functools
import math

import jax
import jax.numpy as jnp
from jax import lax
from jax.experimental import pallas as pl
from jax.experimental.pallas import tpu as pltpu

F32 = jnp.float32
BF16 = jnp.bfloat16

META_LEN = 16
GLA_HEADS = 4
GLA_DK = 64
GLA_DV = 128
GLA_RANK = 16
GLA_TAU = 16.0
DIFF_HEADS = 6
DIFF_DH = 64
SB_HEADS = 6
SB_DH = 128
HEAD_W = 128
EPS = 1e-6
NEG = -1e30
EXP_UNDERFLOW = -110.0

GLA_WIDTH = GLA_HEADS * GLA_DV
ATT_WIDTH = DIFF_HEADS * HEAD_W
IN_WIDTHS = (256, 256, 512, 16, 512, 768, 768, 768, 768, 768, 768, 768, 768)

COL_TILE = ATT_WIDTH
HEADS_PER_TILE = COL_TILE // HEAD_W
RANK_PAD = 256
PG_WIDTH = 3 * COL_TILE
PG_Q, PG_K, PG_V, PG_G, PG_R = 0, 256, 512, 1024, 1536
PG_USED = PG_R + RANK_PAD
SEGMENTS_A = (("pg", 3), ("dq", 1), ("dk", 1), ("dv", 1), ("dg", 1))
SEGMENTS_B = (("sq", 1), ("sk", 1), ("sv", 1), ("sg", 1))
N_COL_TILES = sum(n for _, n in SEGMENTS_A + SEGMENTS_B)

VMEM_LIMIT = 56 * 1024 * 1024
DIFF_KEY_CHUNK = 512
SB_KEY_CHUNK = 256
SB_HEADS_PER_STEP = 3
PAGES_PER_STEP = 8
MAX_REST_PAGES_PER_STEP = 14


def _cparams(sem):
    return pltpu.CompilerParams(dimension_semantics=sem, vmem_limit_bytes=VMEM_LIMIT)


def _softplus(z):
    return jnp.maximum(z, 0.0) + jnp.log1p(jnp.exp(-jnp.abs(z)))


def _log_sigmoid(z):
    return -_softplus(-z)


def _silu(g):
    return g / (1.0 + jnp.exp(-g))


def _nt_dot(a, b):
    return lax.dot_general(a, b, (((1,), (1,)), ((), ())), preferred_element_type=F32)


def _dot(a, b):
    return jnp.dot(a, b, preferred_element_type=F32)


def _rms(o, g):
    return o * lax.rsqrt(jnp.mean(o * o, axis=-1, keepdims=True) + EPS) * g


def _pack_w_in(w):
    offs = [0]
    for wd in IN_WIDTHS:
        offs.append(offs[-1] + wd)
    c = [w[:, offs[i]:offs[i + 1]] for i in range(len(IN_WIDTHS))]
    gq, gk, gv, gr, gg = c[:5]
    pad = jnp.zeros((w.shape[0], PG_WIDTH - PG_R - GLA_RANK), w.dtype)
    packed = jnp.concatenate([gq, gk, gv, gg, gr, pad] + c[5:], axis=1).astype(BF16)
    return packed.reshape(w.shape[0], N_COL_TILES, COL_TILE).transpose(1, 0, 2)


def _inproj_kernel(x_ref, g_ref, w_ref, *refs, t_valid, tm, outs):
    out_refs = refs[:len(outs)]
    h_ref = refs[len(outs)]
    i = pl.program_id(0)
    j = pl.program_id(1)

    @pl.when(j == 0)
    def _():
        x = x_ref[...]
        y = x * lax.rsqrt(jnp.mean(x * x, axis=-1, keepdims=True) + EPS) * g_ref[...]
        rows = i * tm + lax.broadcasted_iota(jnp.int32, (tm, 1), 0)
        h_ref[...] = jnp.where(rows < t_valid, y, 0.0).astype(BF16)

    segments = sorted({(start, n) for start, n, _ in outs})
    for start, n in segments:
        @pl.when((j >= start) & (j < start + n))
        def _(start=start, n=n):
            acc = _dot(h_ref[...], w_ref[0])
            for o_ref, (o_start, _, kind) in zip(out_refs, outs):
                if o_start != start:
                    continue
                if kind == "rows":
                    o_ref[...] = acc.astype(o_ref.dtype)
                    continue
                for hh in range(HEADS_PER_TILE):
                    a = acc[:, HEAD_W * hh:HEAD_W * (hh + 1)]
                    if kind == "heads":
                        o_ref[hh] = a.astype(o_ref.dtype)
                    else:
                        at = a.T
                        tw = o_ref.shape[-1]
                        for c in range(tm // tw):
                            o_ref[hh, c] = at[:, c * tw:(c + 1) * tw].astype(o_ref.dtype)


def _inproj(x, g, w_tiles, *, t_pad, tm, segments, tile0, value_chunk):
    t_valid, d = x.shape
    assert t_pad % tm == 0
    spec = []
    start = 0
    for name, n in segments:
        if name.endswith("g"):
            spec.append((name, start, n, "rows", F32, t_pad, 0))
        elif name.endswith("q"):
            spec.append((name, start, n, "heads", BF16, t_pad, 0))
        else:
            spec.append((name, start, n, "heads", F32, t_valid, 0))
            if value_chunk is not None and name.endswith("k"):
                spec.append((name + "b", start, n, "heads", BF16, t_pad, 0))
            if value_chunk is not None and name.endswith("v"):
                assert tm % value_chunk == 0
                spec.append((name + "t", start, n, "heads_t", BF16, t_pad, value_chunk))
        start += n
    n_tiles = start
    out_shapes, out_specs, outs = [], [], []
    for key, first, n, kind, dt, rows, tw in spec:
        outs.append((first, n, kind))
        if kind == "heads":
            assert n == 1
            out_shapes.append(jax.ShapeDtypeStruct((HEADS_PER_TILE, rows, HEAD_W), dt))
            out_specs.append(pl.BlockSpec((HEADS_PER_TILE, tm, HEAD_W), lambda i, j: (0, i, 0)))
        elif kind == "heads_t":
            assert n == 1
            out_shapes.append(jax.ShapeDtypeStruct((HEADS_PER_TILE, rows // tw, HEAD_W, tw), dt))
            out_specs.append(pl.BlockSpec((HEADS_PER_TILE, tm // tw, HEAD_W, tw), lambda i, j: (0, i, 0, 0)))
        else:
            out_shapes.append(jax.ShapeDtypeStruct((rows, n * COL_TILE), dt))
            out_specs.append(pl.BlockSpec(
                (tm, COL_TILE), lambda i, j, first=first, n=n: (i, jnp.clip(j - first, 0, n - 1))))
    res = pl.pallas_call(
        functools.partial(_inproj_kernel, t_valid=t_valid, tm=tm, outs=tuple(outs)),
        out_shape=out_shapes,
        grid=(t_pad // tm, n_tiles),
        in_specs=[pl.BlockSpec((tm, d), lambda i, j: (i, 0)),
                  pl.BlockSpec((1, d), lambda i, j: (0, 0)),
                  pl.BlockSpec((1, d, COL_TILE), lambda i, j: (tile0 + j, 0, 0))],
        out_specs=out_specs,
        scratch_shapes=[pltpu.VMEM((tm, d), BF16)],
        compiler_params=_cparams(("arbitrary", "arbitrary")),
        name="inproj",
    )(x, g.reshape(1, d), w_tiles)
    return {key: r for (key, *_), r in zip(spec, res)}


def _inproj_all(x, g, w_tiles, *, t_pad, tm, value_chunks):
    n_a = sum(n for _, n in SEGMENTS_A)
    vc = value_chunks or {"d": None, "s": None}
    out = _inproj(x, g, w_tiles, t_pad=t_pad, tm=tm, segments=SEGMENTS_A, tile0=0, value_chunk=vc["d"])
    out.update(_inproj(x, g, w_tiles, t_pad=t_pad, tm=tm, segments=SEGMENTS_B, tile0=n_a, value_chunk=vc["s"]))
    return out


def _cumsum_rows(lg, n):
    ri = lax.broadcasted_iota(jnp.int32, (n, n), 0)
    ci = lax.broadcasted_iota(jnp.int32, (n, n), 1)
    tri = jnp.where(ri >= ci, 1.0, 0.0).astype(BF16)
    hi = lg.astype(BF16)
    lo = (lg - hi.astype(F32)).astype(BF16)
    return _dot(tri, hi) + _dot(tri, lo)


def _gla_prompt_kernel(pg_ref, wg_ref, bg_ref, gn_ref, og_ref, sfin_ref, s_ref, *, t_valid, chunk, sub):
    step = pl.program_id(0)

    @pl.when(step == 0)
    def _():
        s_ref[...] = jnp.zeros_like(s_ref)

    rows = step * chunk + lax.broadcasted_iota(jnp.int32, (chunk, 1), 0)
    x = _dot(pg_ref[:, PG_R:PG_R + RANK_PAD].astype(BF16), wg_ref[...]) + bg_ref[...]
    lg = jnp.where(rows < t_valid, _log_sigmoid(x) * (1.0 / GLA_TAU), 0.0)
    b = _cumsum_rows(lg, chunk)

    lane = lax.broadcasted_iota(jnp.int32, (1, HEAD_W), 1)
    head_mask = (lane < GLA_DK, lane >= GLA_DK)
    n_sub = chunk // sub
    ri = lax.broadcasted_iota(jnp.int32, (sub, chunk), 0)
    ci = lax.broadcasted_iota(jnp.int32, (sub, chunk), 1)
    krow = lax.broadcasted_iota(jnp.int32, (chunk, 1), 0)

    pair_work = []
    for p in range(2):
        qp = pg_ref[:, PG_Q + HEAD_W * p:PG_Q + HEAD_W * (p + 1)] * (GLA_DK ** -0.5)
        kp = pg_ref[:, PG_K + HEAD_W * p:PG_K + HEAD_W * (p + 1)]
        bp = b[:, HEAD_W * p:HEAD_W * (p + 1)]
        s_old = s_ref[p]
        s_bf = s_old.astype(BF16)
        qe = qp * jnp.exp(bp)
        b_end = bp[chunk - 1:chunk]
        kend_t = (kp * jnp.exp(b_end - bp)).T
        dec_t = jnp.broadcast_to(jnp.exp(b_end), (HEAD_W, HEAD_W)).T
        q_blk, k_blk = [], []
        for blk in range(n_sub):
            lo, hi = blk * sub, (blk + 1) * sub
            r = jnp.zeros((1, HEAD_W), F32) if blk == 0 else bp[lo - 1:lo]
            q_blk.append(qp[lo:hi] * jnp.exp(bp[lo:hi] - r))
            k_blk.append((kp * jnp.exp(jnp.where(krow < hi, r - bp, 0.0))).astype(BF16))
        upd, o_inter, att_raw, vbs = [], [], [], []
        for hh in range(2):
            h = 2 * p + hh
            vb = pg_ref[:, PG_V + GLA_DV * h:PG_V + GLA_DV * (h + 1)].astype(BF16)
            vbs.append(vb)
            o_inter.append(_dot(jnp.where(head_mask[hh], qe, 0.0).astype(BF16), s_bf))
            upd.append(_dot(kend_t[GLA_DK * hh:GLA_DK * (hh + 1)].astype(BF16), vb))
            att_raw.append([_nt_dot(jnp.where(head_mask[hh], q_blk[blk], 0.0).astype(BF16), k_blk[blk])
                            for blk in range(n_sub)])
        s_ref[p] = dec_t * s_old + jnp.concatenate(upd, axis=0)
        pair_work.append((o_inter, att_raw, vbs))

    for p in range(2):
        o_inter, att_raw, vbs = pair_work[p]
        for hh in range(2):
            h = 2 * p + hh
            att = [jnp.where(ci <= ri + blk * sub, att_raw[hh][blk], 0.0).astype(BF16) for blk in range(n_sub)]
            o = o_inter[hh] + jnp.concatenate([_dot(a, vbs[hh]) for a in att], axis=0)
            og_ref[:, GLA_DV * h:GLA_DV * (h + 1)] = _rms(o, gn_ref[...])

    @pl.when(step == pl.num_programs(0) - 1)
    def _():
        sfin_ref[...] = s_ref[...]


def _gla_prompt(pg, wg_pad, bg, gn, *, t_valid, chunk=128, sub=16):
    t_pad = pg.shape[0]
    n_steps = pl.cdiv(t_valid, chunk)
    assert n_steps * chunk <= t_pad
    og, sfin = pl.pallas_call(
        functools.partial(_gla_prompt_kernel, t_valid=t_valid, chunk=chunk, sub=sub),
        out_shape=[jax.ShapeDtypeStruct((n_steps * chunk, GLA_WIDTH), F32),
                   jax.ShapeDtypeStruct((2, HEAD_W, GLA_DV), F32)],
        grid=(n_steps,),
        in_specs=[pl.BlockSpec((chunk, PG_USED), lambda i: (i, 0)),
                  pl.BlockSpec((RANK_PAD, GLA_HEADS * GLA_DK), lambda i: (0, 0)),
                  pl.BlockSpec((1, GLA_HEADS * GLA_DK), lambda i: (0, 0)),
                  pl.BlockSpec((1, GLA_DV), lambda i: (0, 0))],
        out_specs=[pl.BlockSpec((chunk, GLA_WIDTH), lambda i: (i, 0)),
                   pl.BlockSpec((2, HEAD_W, GLA_DV), lambda i: (0, 0, 0))],
        scratch_shapes=[pltpu.VMEM((2, HEAD_W, GLA_DV), F32)],
        compiler_params=_cparams(("arbitrary",)),
        name="gla_prompt",
    )(pg, wg_pad, bg, gn)
    return og, sfin.reshape(GLA_HEADS, GLA_DK, GLA_DV)


def _gla_sample_kernel(pg_ref, st_ref, wg_ref, bg_ref, gn_ref, og_ref, snew_ref, *, n_tok):
    pad = HEAD_W - n_tok
    zpad = jnp.zeros((pad, HEAD_W), F32)
    gr = jnp.concatenate([pg_ref[0, :, PG_R:PG_R + RANK_PAD], jnp.zeros((pad, RANK_PAD), F32)], axis=0)
    x = _dot(gr.astype(BF16), wg_ref[...]) + bg_ref[...]
    rows = lax.broadcasted_iota(jnp.int32, (HEAD_W, 1), 0)
    lg = jnp.where(rows < n_tok, _log_sigmoid(x) * (1.0 / GLA_TAU), 0.0)
    b = _cumsum_rows(lg, HEAD_W)

    lane = lax.broadcasted_iota(jnp.int32, (1, HEAD_W), 1)
    head_mask = (lane < GLA_DK, lane >= GLA_DK)
    ri = lax.broadcasted_iota(jnp.int32, (HEAD_W, HEAD_W), 0)
    ci = lax.broadcasted_iota(jnp.int32, (HEAD_W, HEAD_W), 1)
    og = []
    for p in range(2):
        qp = jnp.concatenate([pg_ref[0, :,PG_Q + HEAD_W * p:PG_Q + HEAD_W * (p + 1)], zpad], axis=0) * (GLA_DK ** -0.5)
        kp = jnp.concatenate([pg_ref[0, :,PG_K + HEAD_W * p:PG_K + HEAD_W * (p + 1)], zpad], axis=0)
        bp = b[:, HEAD_W * p:HEAD_W * (p + 1)]
        s_old = jnp.concatenate([st_ref[0, 0, 2 * p], st_ref[0, 0, 2 * p + 1]], axis=0)
        s_bf = s_old.astype(BF16)
        qe = qp * jnp.exp(bp)
        kinv = (kp * jnp.exp(-bp)).astype(BF16)
        b_end = bp[HEAD_W - 1:HEAD_W]
        kend_t = (kp * jnp.exp(b_end - bp)).T
        dec_t = jnp.broadcast_to(jnp.exp(b_end), (HEAD_W, HEAD_W)).T
        upd = []
        for hh in range(2):
            h = 2 * p + hh
            vb = jnp.concatenate([pg_ref[0, :,PG_V + GLA_DV * h:PG_V + GLA_DV * (h + 1)], zpad], axis=0).astype(BF16)
            qm = jnp.where(head_mask[hh], qe, 0.0).astype(BF16)
            att = jnp.where(ci <= ri, _nt_dot(qm, kinv), 0.0)
            o = _dot(qm, s_bf) + _dot(att.astype(BF16), vb)
            og.append(_rms(o, gn_ref[...])[:n_tok])
            upd.append(_dot(kend_t[GLA_DK * hh:GLA_DK * (hh + 1)].astype(BF16), vb))
        s_new = dec_t * s_old + jnp.concatenate(upd, axis=0)
        snew_ref[0, 2 * p] = s_new[:GLA_DK]
        snew_ref[0, 2 * p + 1] = s_new[GLA_DK:]
    og_ref[0] = jnp.concatenate(og, axis=1)


def _gla_sample(pg, state, layer, wg_pad, bg, gn, *, n_seq, n_tok):
    pg3 = pg.reshape(n_seq, n_tok, PG_WIDTH)
    og, snew = pl.pallas_call(
        functools.partial(_gla_sample_kernel, n_tok=n_tok),
        out_shape=[jax.ShapeDtypeStruct((n_seq, n_tok, GLA_WIDTH), F32),
                   jax.ShapeDtypeStruct((n_seq, GLA_HEADS, GLA_DK, GLA_DV), F32)],
        grid=(n_seq,),
        in_specs=[pl.BlockSpec((1, n_tok, PG_USED), lambda i: (i, 0, 0)),
                  pl.BlockSpec((1, 1, GLA_HEADS, GLA_DK, GLA_DV), lambda i: (layer, i, 0, 0, 0)),
                  pl.BlockSpec((RANK_PAD, GLA_HEADS * GLA_DK), lambda i: (0, 0)),
                  pl.BlockSpec((1, GLA_HEADS * GLA_DK), lambda i: (0, 0)),
                  pl.BlockSpec((1, GLA_DV), lambda i: (0, 0))],
        out_specs=[pl.BlockSpec((1, n_tok, GLA_WIDTH), lambda i: (i, 0, 0)),
                   pl.BlockSpec((1, GLA_HEADS, GLA_DK, GLA_DV), lambda i: (i, 0, 0, 0))],
        compiler_params=_cparams(("arbitrary",)),
        name="gla_sample",
    )(pg3, state, wg_pad, bg, gn)
    return og.reshape(n_seq * n_tok, GLA_WIDTH), snew


def _diff_lambda(lqk_ref, lam_init):
    s1 = jnp.sum(lqk_ref[0:1, :] * lqk_ref[1:2, :], axis=-1, keepdims=True)
    s2 = jnp.sum(lqk_ref[2:3, :] * lqk_ref[3:4, :], axis=-1, keepdims=True)
    return jnp.exp(s1) - jnp.exp(s2) + lam_init


def _split_maps(q):
    lane = lax.broadcasted_iota(jnp.int32, q.shape, 1)
    zero = jnp.zeros_like(q)
    return jnp.concatenate([jnp.where(lane < DIFF_DH, q, zero), jnp.where(lane >= DIFF_DH, q, zero)], axis=0)


def _diff_prompt_kernel(lqk_ref, dn_ref, q_ref, k_ref, vt_ref, o_ref, m_ref, l_ref, acc_ref, s_ref, *, bq, bk,
                        lam_init):
    i = pl.program_id(1)
    qq = _split_maps(q_ref[0]) * (DIFF_DH ** -0.5)
    lane = lax.broadcasted_iota(jnp.int32, (1, 2 * bq), 1)
    qpos = i * bq + jnp.where(lane < bq, lane, lane - bq)
    m_ref[...] = jnp.full_like(m_ref, NEG)
    l_ref[...] = jnp.zeros_like(l_ref)
    acc_ref[...] = jnp.zeros_like(acc_ref)

    n_full = (i * bq + 1) // bk
    n_need = (i * bq + bq + bk - 1) // bk

    def scores(j):
        off = pl.multiple_of(j * bk, bk)
        return _nt_dot(k_ref[0, pl.ds(off, bk), :], qq)

    s_ref[...] = scores(0)

    def chunk(j, masked):
        s = s_ref[...]
        s_next = scores(jnp.minimum(j + 1, n_need - 1))
        if masked:
            kpos = j * bk + lax.broadcasted_iota(jnp.int32, (bk, 1), 0)
            s = jnp.where(kpos <= qpos, s, NEG)
        m_old = m_ref[...]
        m_new = jnp.maximum(m_old, jnp.max(s, axis=0, keepdims=True))
        alpha = jnp.exp(m_old - m_new)
        p = jnp.exp(s - m_new)
        l_ref[...] = alpha * l_ref[...] + jnp.sum(p, axis=0, keepdims=True)
        acc_ref[...] = alpha * acc_ref[...] + _dot(vt_ref[0, j], p.astype(BF16))
        m_ref[...] = m_new
        s_ref[...] = s_next

    def full_body(j, c):
        chunk(j, False)
        return c

    def diag_body(j, c):
        chunk(j, True)
        return c

    def pair_body(t, c):
        chunk(2 * t, False)
        chunk(2 * t + 1, False)
        return c

    n_pairs = n_full // 2
    lax.fori_loop(0, n_pairs, pair_body, 0)
    lax.fori_loop(2 * n_pairs, n_full, full_body, 0)
    lax.fori_loop(n_full, n_need, diag_body, 0)

    lam = _diff_lambda(lqk_ref, lam_init)
    on = acc_ref[...] / l_ref[...]
    o = (on[:, :bq] - lam * on[:, bq:]).T
    o_ref[...] = _rms(o, dn_ref[...]) * (1.0 - lam_init)


def _diff_prompt(q, k, vt, lqk, dn, *, t_valid, lam_init, bq=256):
    n_h, t_pad, _ = q.shape
    bk = vt.shape[-1]
    nq = pl.cdiv(t_valid, bq)
    assert pl.cdiv(nq * bq, bk) * bk <= t_pad
    return pl.pallas_call(
        functools.partial(_diff_prompt_kernel, bq=bq, bk=bk, lam_init=lam_init),
        out_shape=jax.ShapeDtypeStruct((nq * bq, n_h * HEAD_W), F32),
        grid=(n_h, nq),
        in_specs=[pl.BlockSpec((4, DIFF_DH), lambda h, i: (0, 0)),
                  pl.BlockSpec((1, HEAD_W), lambda h, i: (0, 0)),
                  pl.BlockSpec((1, bq, HEAD_W), lambda h, i: (h, i, 0)),
                  pl.BlockSpec((1, t_pad, HEAD_W), lambda h, i: (h, 0, 0)),
                  pl.BlockSpec((1, t_pad // bk, HEAD_W, bk), lambda h, i: (h, 0, 0, 0))],
        out_specs=pl.BlockSpec((bq, HEAD_W), lambda h, i: (i, h)),
        scratch_shapes=[pltpu.VMEM((1, 2 * bq), F32), pltpu.VMEM((1, 2 * bq), F32),
                        pltpu.VMEM((HEAD_W, 2 * bq), F32), pltpu.VMEM((bk, 2 * bq), F32)],
        compiler_params=_cparams(("arbitrary", "arbitrary")),
        name="diff_prompt",
    )(lqk, dn, q, k, vt)


def _bf16_round(x):
    return x.astype(BF16).astype(F32)


def _diff_sample_kernel(pt_ref, lqk_ref, dn_ref, q_ref, kn_ref, vn_ref, *refs, n_tok, pages_per_step, lam_init):
    g_pages = pages_per_step
    k_refs = refs[:g_pages]
    v_refs = refs[g_pages:2 * g_pages]
    o_ref = refs[2 * g_pages]
    qq_ref, m_ref, l_ref, acc_ref = refs[2 * g_pages + 1:]
    n_h = q_ref.shape[1]
    n2 = 2 * n_tok
    p = pl.program_id(1)

    @pl.when(p == 0)
    def _():
        qrow = lax.broadcasted_iota(jnp.int32, (n2, 1), 0) % n_tok
        for h in range(n_h):
            qq = _split_maps(q_ref[0, h]) * (DIFF_DH ** -0.5)
            qq_ref[h] = qq.astype(F32)
            qf = qq.astype(F32)
            kn = _bf16_round(kn_ref[0, h])
            vn = _bf16_round(vn_ref[0, h])
            s = [jnp.where(qrow >= t, jnp.sum(qf * kn[t:t + 1], axis=-1, keepdims=True), NEG) for t in range(n_tok)]
            m = s[0]
            for t in range(1, n_tok):
                m = jnp.maximum(m, s[t])
            l = jnp.zeros((n2, 1), F32)
            acc = jnp.zeros((n2, HEAD_W), F32)
            for t in range(n_tok):
                pt = jnp.exp(s[t] - m)
                l = l + pt
                acc = acc + _bf16_round(pt) * vn[t:t + 1]
            m_ref[h] = m
            l_ref[h] = l
            acc_ref[h] = acc

    page = k_refs[0].shape[3]
    scores = []
    for h in range(n_h):
        qh = qq_ref[h].astype(BF16)
        scores.append(jnp.concatenate(
            [_nt_dot(qh, k_refs[g][0, 0, h].astype(BF16)) for g in range(g_pages)], axis=1))
    probs, alphas = [], []
    for h in range(n_h):
        s = scores[h]
        m_old = m_ref[h]
        m_new = jnp.maximum(m_old, jnp.max(s, axis=-1, keepdims=True))
        alpha = jnp.exp(m_old - m_new)
        p32 = jnp.exp(s - m_new)
        l_ref[h] = alpha * l_ref[h] + jnp.sum(p32, axis=-1, keepdims=True)
        m_ref[h] = m_new
        probs.append(p32.astype(BF16))
        alphas.append(alpha)
    for h in range(n_h):
        pv = _dot(probs[h][:, :page], v_refs[0][0, 0, h].astype(BF16))
        for g in range(1, g_pages):
            pv += _dot(probs[h][:, g * page:(g + 1) * page], v_refs[g][0, 0, h].astype(BF16))
        acc_ref[h] = alphas[h] * acc_ref[h] + pv

    @pl.when(p == pl.num_programs(1) - 1)
    def _():
        lam = _diff_lambda(lqk_ref, lam_init)
        outs = []
        for h in range(n_h):
            on = acc_ref[h] / l_ref[h]
            o = on[:n_tok] - lam * on[n_tok:]
            outs.append(_rms(o, dn_ref[...]) * (1.0 - lam_init))
        o_ref[0] = jnp.concatenate(outs, axis=1)


def _page_specs(layer, n_pages, pages_per_step, n_h, page, reverse):
    specs = []
    for g in range(pages_per_step):
        def imap(b, p, pt, g=g):
            idx = p * pages_per_step + g
            if reverse:
                idx = n_pages - 1 - idx
            return (layer, pt[b * n_pages + idx], 0, 0, 0)
        specs.append(pl.BlockSpec((1, 1, n_h, page, HEAD_W), imap))
    return specs


def _diff_sample(q, kn, vn, cache_k, cache_v, page_table, layer, lqk, dn, *, lam_init, pages_per_step=PAGES_PER_STEP):
    n_seq, n_h, n_tok, _ = q.shape
    n_pages = page_table.shape[1]
    page = cache_k.shape[3]
    assert n_pages % pages_per_step == 0
    tok_spec = pl.BlockSpec((1, n_h, n_tok, HEAD_W), lambda b, p, pt: (b, 0, 0, 0))
    pspecs = _page_specs(layer, n_pages, pages_per_step, n_h, page, reverse=False)
    out = pl.pallas_call(
        functools.partial(_diff_sample_kernel, n_tok=n_tok, pages_per_step=pages_per_step, lam_init=lam_init),
        out_shape=jax.ShapeDtypeStruct((n_seq, n_tok, n_h * HEAD_W), F32),
        grid_spec=pltpu.PrefetchScalarGridSpec(
            num_scalar_prefetch=1,
            grid=(n_seq, n_pages // pages_per_step),
            in_specs=[pl.BlockSpec((4, DIFF_DH), lambda b, p, pt: (0, 0)),
                      pl.BlockSpec((1, HEAD_W), lambda b, p, pt: (0, 0)),
                      tok_spec, tok_spec, tok_spec] + pspecs + pspecs,
            out_specs=pl.BlockSpec((1, n_tok, n_h * HEAD_W), lambda b, p, pt: (b, 0, 0)),
            scratch_shapes=[pltpu.VMEM((n_h, 2 * n_tok, HEAD_W), F32), pltpu.VMEM((n_h, 2 * n_tok, 1), F32),
                            pltpu.VMEM((n_h, 2 * n_tok, 1), F32), pltpu.VMEM((n_h, 2 * n_tok, HEAD_W), F32)],
        ),
        compiler_params=_cparams(("arbitrary", "arbitrary")),
        name="diff_sample",
    )(page_table.reshape(-1), lqk, dn, q, kn, vn, *([cache_k] * pages_per_step), *([cache_v] * pages_per_step))
    return out.reshape(n_seq * n_tok, n_h * HEAD_W)


def _later_keys_matrix(n):
    ri = lax.broadcasted_iota(jnp.int32, (n, n), 0)
    ci = lax.broadcasted_iota(jnp.int32, (n, n), 1)
    return jnp.where(ri > ci, 1.0, 0.0).astype(BF16)


def _tail_sums(lk, u):
    n = lk.shape[0]
    hi = lk.astype(BF16)
    lo = (lk - hi.astype(F32)).astype(BF16)
    t = _dot(jnp.concatenate([hi, lo], axis=0), u)
    return t[:n] + t[n:]


def _sb_prompt_kernel(q_ref, k_ref, vt_ref, o_ref, c_ref, acc_ref, *, blk, heads):
    i = pl.program_id(1)
    ri = lax.broadcasted_iota(jnp.int32, (blk, blk), 0)
    ci = lax.broadcasted_iota(jnp.int32, (blk, blk), 1)
    later = jnp.where(ci > ri, 1.0, 0.0).astype(BF16)
    qpos = i * blk + lax.broadcasted_iota(jnp.int32, (1, blk), 1)
    c_ref[...] = jnp.zeros_like(c_ref)
    acc_ref[...] = jnp.zeros_like(acc_ref)

    def chunk(j, masked):
        off = pl.multiple_of(j * blk, blk)
        zs = [_nt_dot(k_ref[h, pl.ds(off, blk), :], q_ref[h]) * (SB_DH ** -0.5) for h in range(heads)]
        if masked:
            kpos = j * blk + lax.broadcasted_iota(jnp.int32, (blk, 1), 0)
            vis = kpos < qpos
        lbs, lks, tails = [], [], []
        for h in range(heads):
            sp = _softplus(zs[h])
            lk = -sp
            lbs.append(zs[h] - sp)
            if masked:
                lk = jnp.where(vis, lk, 0.0)
            lks.append(lk)
            hi = lk.astype(BF16)
            lo = (lk - hi.astype(F32)).astype(BF16)
            t = _dot(later, jnp.concatenate([hi, lo], axis=1))
            tails.append(t[:, :blk] + t[:, blk:])
        c_max = None
        for h in range(heads):
            c = c_ref[h]
            a = jnp.exp(lbs[h] + tails[h] + c)
            if masked:
                a = jnp.where(vis, a, 0.0)
            acc_ref[h] += _dot(vt_ref[h, j], a.astype(BF16))
            c_new = c + jnp.sum(lks[h], axis=0, keepdims=True)
            c_ref[h] = c_new
            m = jnp.max(c_new)
            c_max = m if c_max is None else jnp.maximum(c_max, m)
        return c_max

    def cond(carry):
        t, c_max = carry
        return jnp.logical_and(t < i, c_max > EXP_UNDERFLOW)

    def body(carry):
        t, _ = carry
        return t + 1, chunk(i - 1 - t, False)

    lax.while_loop(cond, body, (jnp.int32(0), chunk(i, True)))
    for h in range(heads):
        o_ref[:, HEAD_W * h:HEAD_W * (h + 1)] = acc_ref[h].T


def _sb_prompt(q, k, vt, *, t_valid, heads=SB_HEADS_PER_STEP):
    n_h, t_pad, _ = q.shape
    blk = vt.shape[-1]
    nq = pl.cdiv(t_valid, blk)
    assert nq * blk <= t_pad and n_h % heads == 0
    return pl.pallas_call(
        functools.partial(_sb_prompt_kernel, blk=blk, heads=heads),
        out_shape=jax.ShapeDtypeStruct((nq * blk, n_h * HEAD_W), F32),
        grid=(n_h // heads, nq),
        in_specs=[pl.BlockSpec((heads, blk, HEAD_W), lambda h, i: (h, i, 0)),
                  pl.BlockSpec((heads, t_pad, HEAD_W), lambda h, i: (h, 0, 0)),
                  pl.BlockSpec((heads, t_pad // blk, HEAD_W, blk), lambda h, i: (h, 0, 0, 0))],
        out_specs=pl.BlockSpec((blk, heads * HEAD_W), lambda h, i: (i, h)),
        scratch_shapes=[pltpu.VMEM((heads, 1, blk), F32), pltpu.VMEM((heads, HEAD_W, blk), F32)],
        compiler_params=_cparams(("arbitrary", "arbitrary")),
        name="sb_prompt",
    )(q, k, vt)


def _sb_new_tokens(q8, kn, vn, n_tok):
    rows8 = q8.shape[0]
    qrow = lax.broadcasted_iota(jnp.int32, (rows8, 1), 0)
    c = jnp.zeros((rows8, 1), F32)
    acc = jnp.zeros((rows8, HEAD_W), F32)
    for t in range(n_tok - 1, -1, -1):
        z = jnp.sum(q8 * kn[t:t + 1], axis=-1, keepdims=True) * (SB_DH ** -0.5)
        sp = _softplus(z)
        vis = qrow > t
        a = jnp.where(vis, jnp.exp(z - sp + c), 0.0)
        acc = acc + _bf16_round(a) * vn[t:t + 1]
        c = c + jnp.where(vis, -sp, 0.0)
    return c, acc


def _sb_sample_kernel(pt_ref, done_ref, q_ref, *refs, n_tok, pages_per_step, first_stage):
    g_pages = pages_per_step
    seed_a, seed_b = refs[:2]
    k_refs = refs[2:2 + g_pages]
    v_refs = refs[2 + g_pages:2 + 2 * g_pages]
    n_out = 3 if first_stage else 1
    out_refs = refs[2 + 2 * g_pages:2 + 2 * g_pages + n_out]
    c_ref, acc_ref = refs[2 + 2 * g_pages + n_out:]
    n_h = q_ref.shape[1]
    page = k_refs[0].shape[3]
    rows8 = 8
    b = pl.program_id(0)
    p = pl.program_id(1)
    q8s = [jnp.concatenate([q_ref[0, h].astype(F32), jnp.zeros((rows8 - n_tok, HEAD_W), F32)], axis=0)
           for h in range(n_h)]

    @pl.when(p == 0)
    def _():
        for h in range(n_h):
            if first_stage:
                c, acc = _sb_new_tokens(q8s[h], _bf16_round(seed_a[0, h]), _bf16_round(seed_b[0, h]), n_tok)
            else:
                c, acc = seed_a[0, h], seed_b[0, h]
            c_ref[h] = c
            acc_ref[h] = acc

    def visit_pages():
        _sb_visit_pages(q8s, k_refs, v_refs, c_ref, acc_ref, page)

    if first_stage:
        visit_pages()
    else:
        pl.when(done_ref[b] == 0)(visit_pages)

    @pl.when(p == pl.num_programs(1) - 1)
    def _():
        if first_stage:
            c_out, acc_out, done_out = out_refs
            c_max = jnp.full((1, 1), NEG, F32)
            for h in range(n_h):
                c_out[0, h] = c_ref[h]
                acc_out[0, h] = acc_ref[h]
                c_max = jnp.maximum(c_max, jnp.max(c_ref[h][:n_tok], axis=0, keepdims=True))
            done_out[0] = jnp.where(c_max < EXP_UNDERFLOW, 1, 0).astype(jnp.int32)
        else:
            out_refs[0][0] = jnp.concatenate([acc_ref[h][:n_tok] for h in range(n_h)], axis=1)


def _sb_visit_pages(q8s, k_refs, v_refs, c_ref, acc_ref, page):
    g_pages = len(k_refs)
    n_h = len(q8s)
    rows8 = q8s[0].shape[0]
    u = _later_keys_matrix(page)
    zs = []
    for h in range(n_h):
        qh = q8s[h].astype(BF16)
        zs.append(jnp.concatenate(
            [_nt_dot(qh, k_refs[g][0, 0, h].astype(BF16)) for g in range(g_pages)], axis=0) * (SB_DH ** -0.5))
    lbs, tails, psums = [], [], []
    for h in range(n_h):
        sp = _softplus(zs[h])
        lbs.append(zs[h] - sp)
        tails.append(_tail_sums(-sp, u))
        psums.append(jnp.sum(-sp, axis=-1, keepdims=True))
    for h in range(n_h):
        c = c_ref[h]
        cs = []
        for g in range(g_pages):
            cs.append(c)
            c = c + psums[h][rows8 * g:rows8 * (g + 1)]
        c_ref[h] = c
        a = jnp.exp(lbs[h] + tails[h] + jnp.concatenate(cs, axis=0))
        pv = _dot(a[:rows8].astype(BF16), v_refs[0][0, 0, h].astype(BF16))
        for g in range(1, g_pages):
            pv += _dot(a[rows8 * g:rows8 * (g + 1)].astype(BF16), v_refs[g][0, 0, h].astype(BF16))
        acc_ref[h] += pv


def _sb_sample(q, kn, vn, cache_k, cache_v, page_table, layer, *, pages_per_step=PAGES_PER_STEP):
    n_seq, n_h, n_tok, _ = q.shape
    n_pages = page_table.shape[1]
    page = cache_k.shape[3]
    g_pages = pages_per_step
    assert n_pages % g_pages == 0 and n_pages > g_pages and n_tok <= 8
    pt = page_table.reshape(-1)
    tok_spec = pl.BlockSpec((1, n_h, n_tok, HEAD_W), lambda b, p, pt, done: (b, 0, 0, 0))
    c_spec = pl.BlockSpec((1, n_h, 8, 1), lambda b, p, pt, done: (b, 0, 0, 0))
    acc_spec = pl.BlockSpec((1, n_h, 8, HEAD_W), lambda b, p, pt, done: (b, 0, 0, 0))
    scratch = [pltpu.VMEM((n_h, 8, 1), F32), pltpu.VMEM((n_h, 8, HEAD_W), F32)]

    def page_specs(first_idx, skip_done, per_step):
        specs = []
        for g in range(per_step):
            def imap(b, p, pt, done, g=g):
                phys = pt[b * n_pages + (n_pages - 1 - (first_idx + p * per_step + g))]
                if skip_done:
                    phys = jnp.where(done[b] == 0, phys, 0)
                return (layer, phys, 0, 0, 0)
            specs.append(pl.BlockSpec((1, 1, n_h, page, HEAD_W), imap))
        return specs

    specs1 = page_specs(0, False, g_pages)
    c1, acc1, done = pl.pallas_call(
        functools.partial(_sb_sample_kernel, n_tok=n_tok, pages_per_step=g_pages, first_stage=True),
        out_shape=[jax.ShapeDtypeStruct((n_seq, n_h, 8, 1), F32),
                   jax.ShapeDtypeStruct((n_seq, n_h, 8, HEAD_W), F32),
                   jax.ShapeDtypeStruct((n_seq, 1, 1), jnp.int32)],
        grid_spec=pltpu.PrefetchScalarGridSpec(
            num_scalar_prefetch=2,
            grid=(n_seq, 1),
            in_specs=[tok_spec, tok_spec, tok_spec] + specs1 + specs1,
            out_specs=[c_spec, acc_spec, pl.BlockSpec((1, 1, 1), lambda b, p, pt, done: (b, 0, 0))],
            scratch_shapes=scratch,
        ),
        compiler_params=_cparams(("arbitrary", "arbitrary")),
        name="sb_sample_newest",
    )(pt, jnp.zeros((n_seq,), jnp.int32), q, kn, vn, *([cache_k] * g_pages), *([cache_v] * g_pages))

    n_rest = n_pages - g_pages
    g_rest = max(g for g in range(1, MAX_REST_PAGES_PER_STEP + 1) if n_rest % g == 0)
    specs2 = page_specs(g_pages, True, g_rest)
    rest_call = pl.pallas_call(
        functools.partial(_sb_sample_kernel, n_tok=n_tok, pages_per_step=g_rest, first_stage=False),
        out_shape=jax.ShapeDtypeStruct((n_seq, n_tok, n_h * HEAD_W), F32),
        grid_spec=pltpu.PrefetchScalarGridSpec(
            num_scalar_prefetch=2,
            grid=(n_seq, n_rest // g_rest),
            in_specs=[tok_spec, c_spec, acc_spec] + specs2 + specs2,
            out_specs=pl.BlockSpec((1, n_tok, n_h * HEAD_W), lambda b, p, pt, done: (b, 0, 0)),
            scratch_shapes=scratch,
        ),
        compiler_params=_cparams(("arbitrary", "arbitrary")),
        name="sb_sample_rest",
    )

    def visit_rest():
        return rest_call(pt, done.reshape(-1), q, c1, acc1, *([cache_k] * g_rest), *([cache_v] * g_rest))

    def nothing_left():
        return acc1[:, :, :n_tok, :].transpose(0, 2, 1, 3).reshape(n_seq, n_tok, n_h * HEAD_W)

    out = lax.cond(jnp.all(done == 1), nothing_left, visit_rest)
    return out.reshape(n_seq * n_tok, n_h * HEAD_W)


def _merge_kernel(x_ref, og_ref, gg_ref, od_ref, dg_ref, os_ref, sg_ref, w_ref, fg_ref, o_ref, *, final_norm):
    ag = (og_ref[...] * _silu(gg_ref[...])).astype(BF16)
    ad = (od_ref[...] * _silu(dg_ref[...])).astype(BF16)
    asb = (os_ref[...] * _silu(sg_ref[...])).astype(BF16)
    y = _dot(ag, w_ref[0:GLA_WIDTH, :])
    y += _dot(ad, w_ref[GLA_WIDTH:GLA_WIDTH + ATT_WIDTH, :])
    y += _dot(asb, w_ref[GLA_WIDTH + ATT_WIDTH:, :])
    out = x_ref[...] + y
    o_ref[...] = _rms(out, fg_ref[...]) if final_norm else out


def _merge(x, og, pg, od, dg, osb, sg, w_out, final_g, *, tm, final_norm):
    t_valid, d = x.shape
    assert PG_G % GLA_WIDTH == 0
    return pl.pallas_call(
        functools.partial(_merge_kernel, final_norm=final_norm),
        out_shape=jax.ShapeDtypeStruct((t_valid, d), F32),
        grid=(pl.cdiv(t_valid, tm),),
        in_specs=[pl.BlockSpec((tm, d), lambda i: (i, 0)),
                  pl.BlockSpec((tm, GLA_WIDTH), lambda i: (i, 0)),
                  pl.BlockSpec((tm, GLA_WIDTH), lambda i: (i, PG_G // GLA_WIDTH)),
                  pl.BlockSpec((tm, ATT_WIDTH), lambda i: (i, 0)),
                  pl.BlockSpec((tm, ATT_WIDTH), lambda i: (i, 0)),
                  pl.BlockSpec((tm, ATT_WIDTH), lambda i: (i, 0)),
                  pl.BlockSpec((tm, ATT_WIDTH), lambda i: (i, 0)),
                  pl.BlockSpec(w_out.shape, lambda i: (0, 0)),
                  pl.BlockSpec((1, d), lambda i: (0, 0))],
        out_specs=pl.BlockSpec((tm, d), lambda i: (i, 0)),
        compiler_params=_cparams(("arbitrary",)),
        name="merge",
    )(x, og, pg, od, dg, osb, sg, w_out, final_g.reshape(1, d))


def _head_major_to_cache(a):
    return jnp.transpose(a, (1, 0, 2))


def kernel(x_prompt, x_sample, cache_diff_k, cache_diff_v, cache_sb_k, cache_sb_v, state_gla, page_table, meta_tokens, norm_mix, w_in, gla_w_gate, gla_b_gate, gla_norm, diff_lambda_q1, diff_lambda_k1, diff_lambda_q2, diff_lambda_k2, diff_norm, w_out, final_norm):
    n_batch, seq, d = x_prompt.shape
    assert n_batch == 1
    n_seq, n_tok, _ = x_sample.shape
    depth = w_in.shape[0]
    t_valid = seq + META_LEN
    row_tile = 512
    t_pad = pl.cdiv(t_valid, row_tile) * row_tile
    n_s = n_seq * n_tok

    xp = jnp.concatenate([meta_tokens.astype(x_prompt.dtype), x_prompt[0]], axis=0)
    xs = x_sample.reshape(n_s, d)
    caches = [jnp.transpose(c, (0, 1, 3, 2, 4)) for c in (cache_diff_k, cache_diff_v, cache_sb_k, cache_sb_v)]

    p_rows, s_rows = [], []
    for l in range(depth):
        lam_init = 0.8 - 0.6 * math.exp(-0.3 * l)
        last = l == depth - 1
        w_packed = _pack_w_in(w_in[l])
        w_out_bf = w_out[l].astype(BF16)
        wg_pad = jnp.zeros((RANK_PAD, GLA_HEADS * GLA_DK), F32).at[:GLA_RANK].set(gla_w_gate[l]).astype(BF16)
        bg = gla_b_gate[l].reshape(1, -1)
        gn = gla_norm[l].reshape(1, -1)
        dn = diff_norm[l].reshape(1, -1)
        lqk = jnp.stack([diff_lambda_q1[l], diff_lambda_k1[l], diff_lambda_q2[l], diff_lambda_k2[l]])

        pr = _inproj_all(xp, norm_mix[l], w_packed, t_pad=t_pad, tm=row_tile,
                         value_chunks={"d": DIFF_KEY_CHUNK, "s": SB_KEY_CHUNK})
        og, s_fin = _gla_prompt(pr["pg"], wg_pad, bg, gn, t_valid=t_valid)
        od = _diff_prompt(pr["dq"], pr["dkb"], pr["dvt"], lqk, dn, t_valid=t_valid, lam_init=lam_init)
        osb = _sb_prompt(pr["sq"], pr["skb"], pr["svt"], t_valid=t_valid)
        xp = _merge(xp, og, pr["pg"], od, pr["dg"], osb, pr["sg"], w_out_bf, final_norm, tm=row_tile,
                    final_norm=last)
        p_rows.append(tuple(_head_major_to_cache(pr[k])[None] for k in ("dk", "dv", "sk", "sv")) + (s_fin[None],))

        sr = _inproj_all(xs, norm_mix[l], w_packed, t_pad=n_s, tm=n_s, value_chunks=None)
        og_s, s_new = _gla_sample(sr["pg"], state_gla, l, wg_pad, bg, gn, n_seq=n_seq, n_tok=n_tok)
        per_seq = lambda a: a.reshape(a.shape[0], n_seq, n_tok, HEAD_W).transpose(1, 0, 2, 3)
        od_s = _diff_sample(per_seq(sr["dq"]), per_seq(sr["dk"]), per_seq(sr["dv"]), caches[0], caches[1],
                            page_table, l, lqk, dn, lam_init=lam_init)
        os_s = _sb_sample(per_seq(sr["sq"]), per_seq(sr["sk"]), per_seq(sr["sv"]), caches[2], caches[3],
                          page_table, l)
        xs = _merge(xs, og_s, sr["pg"], od_s, sr["dg"], os_s, sr["sg"], w_out_bf, final_norm, tm=n_s,
                    final_norm=last)
        to_cache = lambda a: a.reshape(a.shape[0], n_seq, n_tok, HEAD_W).transpose(1, 2, 0, 3)
        s_rows.append(tuple(to_cache(sr[k]) for k in ("dk", "dv", "sk", "sv")) + (s_new,))

    y_prompt = xp[META_LEN:][None]
    y_sample = xs.reshape(n_seq, n_tok, d)
    stack = lambda rows, i: jnp.stack([r[i] for r in rows])
    return (y_prompt, y_sample,
            stack(p_rows, 0), stack(p_rows, 1), stack(p_rows, 2), stack(p_rows, 3), stack(p_rows, 4),
            stack(s_rows, 0), stack(s_rows, 1), stack(s_rows, 2), stack(s_rows, 3), stack(s_rows, 4))
```

```python
import functools
import math

import jax
import jax.numpy as jnp
from jax import lax
from jax.experimental import pallas as pl
from jax.experimental.pallas import tpu as pltpu

F32 = jnp.float32
BF16 = jnp.bfloat16

META_LEN = 16
GLA_HEADS = 4
GLA_DK = 64
GLA_DV = 128
GLA_RANK = 16
GLA_TAU = 16.0
DIFF_HEADS = 6
DIFF_DH = 64
SB_HEADS = 6
SB_DH = 128
HEAD_W = 128
EPS = 1e-6
NEG = -1e30
EXP_UNDERFLOW = -110.0

GLA_WIDTH = GLA_HEADS * GLA_DV
ATT_WIDTH = DIFF_HEADS * HEAD_W
IN_WIDTHS = (256, 256, 512, 16, 512, 768, 768, 768, 768, 768, 768, 768, 768)

COL_TILE = ATT_WIDTH
HEADS_PER_TILE = COL_TILE // HEAD_W
RANK_PAD = 256
PG_WIDTH = 3 * COL_TILE
PG_Q, PG_K, PG_V, PG_G, PG_R = 0, 256, 512, 1024, 1536
PG_USED = PG_R + RANK_PAD
SEGMENTS_A = (("pg", 3), ("dq", 1), ("dk", 1), ("dv", 1), ("dg", 1))
SEGMENTS_B = (("sq", 1), ("sk", 1), ("sv", 1), ("sg", 1))
N_COL_TILES = sum(n for _, n in SEGMENTS_A + SEGMENTS_B)

VMEM_LIMIT = 56 * 1024 * 1024
DIFF_KEY_CHUNK = 512
SB_KEY_CHUNK = 256
SB_HEADS_PER_STEP = 3
PAGES_PER_STEP = 8
MAX_REST_PAGES_PER_STEP = 14


def _cparams(sem):
    return pltpu.CompilerParams(dimension_semantics=sem, vmem_limit_bytes=VMEM_LIMIT)


def _softplus(z):
    return jnp.maximum(z, 0.0) + jnp.log1p(jnp.exp(-jnp.abs(z)))


def _log_sigmoid(z):
    return -_softplus(-z)


def _silu(g):
    return g / (1.0 + jnp.exp(-g))


def _nt_dot(a, b):
    return lax.dot_general(a, b, (((1,), (1,)), ((), ())), preferred_element_type=F32)


def _dot(a, b):
    return jnp.dot(a, b, preferred_element_type=F32)


def _rms(o, g):
    return o * lax.rsqrt(jnp.mean(o * o, axis=-1, keepdims=True) + EPS) * g


def _pack_w_in(w):
    offs = [0]
    for wd in IN_WIDTHS:
        offs.append(offs[-1] + wd)
    c = [w[:, offs[i]:offs[i + 1]] for i in range(len(IN_WIDTHS))]
    gq, gk, gv, gr, gg = c[:5]
    pad = jnp.zeros((w.shape[0], PG_WIDTH - PG_R - GLA_RANK), w.dtype)
    packed = jnp.concatenate([gq, gk, gv, gg, gr, pad] + c[5:], axis=1).astype(BF16)
    return packed.reshape(w.shape[0], N_COL_TILES, COL_TILE).transpose(1, 0, 2)


def _inproj_kernel(x_ref, g_ref, w_ref, *refs, t_valid, tm, outs, n_aliased):
    refs = refs[n_aliased:]
    out_refs = refs[:len(outs)]
    h_ref = refs[len(outs)]
    i = pl.program_id(0)
    j = pl.program_id(1)

    @pl.when(j == 0)
    def _():
        x = x_ref[...]
        y = x * lax.rsqrt(jnp.mean(x * x, axis=-1, keepdims=True) + EPS) * g_ref[...]
        rows = i * tm + lax.broadcasted_iota(jnp.int32, (tm, 1), 0)
        h_ref[...] = jnp.where(rows < t_valid, y, 0.0).astype(BF16)

    segments = sorted({(start, n) for start, n, _ in outs})
    for start, n in segments:
        @pl.when((j >= start) & (j < start + n))
        def _(start=start, n=n):
            acc = _dot(h_ref[...], w_ref[0])
            for o_ref, (o_start, _, kind) in zip(out_refs, outs):
                if o_start != start:
                    continue
                if kind == "rows":
                    o_ref[...] = acc.astype(o_ref.dtype)
                    continue
                for hh in range(HEADS_PER_TILE):
                    a = acc[:, HEAD_W * hh:HEAD_W * (hh + 1)]
                    if kind == "heads":
                        o_ref[hh] = a.astype(o_ref.dtype)
                    else:
                        at = a.T
                        tw = o_ref.shape[-1]
                        for c in range(tm // tw):
                            o_ref[hh, c] = at[:, c * tw:(c + 1) * tw].astype(o_ref.dtype)


def _inproj(x, g, w_tiles, *, t_pad, tm, segments, tile0, value_chunk, layer_slabs):
    t_valid, d = x.shape
    assert t_pad % tm == 0
    spec = []
    start = 0
    for name, n in segments:
        if name.endswith("g"):
            spec.append((name, start, n, "rows", F32, t_pad, 0))
        elif name.endswith("q"):
            spec.append((name, start, n, "heads", BF16, t_pad, 0))
        else:
            spec.append((name, start, n, "heads", F32, t_valid, 0))
            if value_chunk is not None and name.endswith("k"):
                spec.append((name + "b", start, n, "heads", BF16, t_pad, 0))
            if value_chunk is not None and name.endswith("v"):
                assert tm % value_chunk == 0
                spec.append((name + "t", start, n, "heads_t", BF16, t_pad, value_chunk))
        start += n
    n_tiles = start
    out_shapes, out_specs, outs = [], [], []
    aliased, aliases = [], {}
    for key, first, n, kind, dt, rows, tw in spec:
        outs.append((first, n, kind))
        if kind == "heads" and dt == F32 and layer_slabs is not None:
            assert n == 1
            layer, depth, previous = layer_slabs
            out_shapes.append(jax.ShapeDtypeStruct((depth, HEADS_PER_TILE, rows, HEAD_W), dt))
            out_specs.append(pl.BlockSpec((None, HEADS_PER_TILE, tm, HEAD_W),
                                          lambda i, j, layer=layer: (layer, 0, i, 0)))
            if previous is not None:
                aliases[3 + len(aliased)] = len(out_shapes) - 1
                aliased.append(previous[key])
        elif kind == "heads":
            assert n == 1
            out_shapes.append(jax.ShapeDtypeStruct((HEADS_PER_TILE, rows, HEAD_W), dt))
            out_specs.append(pl.BlockSpec((HEADS_PER_TILE, tm, HEAD_W), lambda i, j: (0, i, 0)))
        elif kind == "heads_t":
            assert n == 1
            out_shapes.append(jax.ShapeDtypeStruct((HEADS_PER_TILE, rows // tw, HEAD_W, tw), dt))
            out_specs.append(pl.BlockSpec((HEADS_PER_TILE, tm // tw, HEAD_W, tw), lambda i, j: (0, i, 0, 0)))
        else:
            out_shapes.append(jax.ShapeDtypeStruct((rows, n * COL_TILE), dt))
            out_specs.append(pl.BlockSpec(
                (tm, COL_TILE), lambda i, j, first=first, n=n: (i, jnp.clip(j - first, 0, n - 1))))
    res = pl.pallas_call(
        functools.partial(_inproj_kernel, t_valid=t_valid, tm=tm, outs=tuple(outs), n_aliased=len(aliased)),
        out_shape=out_shapes,
        grid=(t_pad // tm, n_tiles),
        in_specs=[pl.BlockSpec((tm, d), lambda i, j: (i, 0)),
                  pl.BlockSpec((1, d), lambda i, j: (0, 0)),
                  pl.BlockSpec((1, d, COL_TILE), lambda i, j: (tile0 + j, 0, 0))]
                 + [pl.BlockSpec(memory_space=pl.ANY)] * len(aliased),
        out_specs=out_specs,
        scratch_shapes=[pltpu.VMEM((tm, d), BF16)],
        input_output_aliases=aliases,
        compiler_params=_cparams(("arbitrary", "arbitrary")),
        name="inproj",
    )(x, g.reshape(1, d), w_tiles, *aliased)
    return {key: r for (key, *_), r in zip(spec, res)}


def _inproj_all(x, g, w_tiles, *, t_pad, tm, value_chunks, layer_slabs=None):
    n_a = sum(n for _, n in SEGMENTS_A)
    vc = value_chunks or {"d": None, "s": None}
    out = _inproj(x, g, w_tiles, t_pad=t_pad, tm=tm, segments=SEGMENTS_A, tile0=0, value_chunk=vc["d"],
                  layer_slabs=layer_slabs)
    out.update(_inproj(x, g, w_tiles, t_pad=t_pad, tm=tm, segments=SEGMENTS_B, tile0=n_a, value_chunk=vc["s"],
                       layer_slabs=layer_slabs))
    return out


def _cumsum_rows(lg, n):
    ri = lax.broadcasted_iota(jnp.int32, (n, n), 0)
    ci = lax.broadcasted_iota(jnp.int32, (n, n), 1)
    tri = jnp.where(ri >= ci, 1.0, 0.0).astype(BF16)
    hi = lg.astype(BF16)
    lo = (lg - hi.astype(F32)).astype(BF16)
    return _dot(tri, hi) + _dot(tri, lo)


def _gla_prompt_kernel(pg_ref, wg_ref, bg_ref, gn_ref, og_ref, sfin_ref, s_ref, *, t_valid, chunk, sub):
    step = pl.program_id(0)

    @pl.when(step == 0)
    def _():
        s_ref[...] = jnp.zeros_like(s_ref)

    rows = step * chunk + lax.broadcasted_iota(jnp.int32, (chunk, 1), 0)
    x = _dot(pg_ref[:, PG_R:PG_R + RANK_PAD].astype(BF16), wg_ref[...]) + bg_ref[...]
    lg = jnp.where(rows < t_valid, _log_sigmoid(x) * (1.0 / GLA_TAU), 0.0)
    b = _cumsum_rows(lg, chunk)

    lane = lax.broadcasted_iota(jnp.int32, (1, HEAD_W), 1)
    head_mask = (lane < GLA_DK, lane >= GLA_DK)
    n_sub = chunk // sub
    ri = lax.broadcasted_iota(jnp.int32, (sub, chunk), 0)
    ci = lax.broadcasted_iota(jnp.int32, (sub, chunk), 1)
    krow = lax.broadcasted_iota(jnp.int32, (chunk, 1), 0)

    pair_work = []
    for p in range(2):
        qp = pg_ref[:, PG_Q + HEAD_W * p:PG_Q + HEAD_W * (p + 1)] * (GLA_DK ** -0.5)
        kp = pg_ref[:, PG_K + HEAD_W * p:PG_K + HEAD_W * (p + 1)]
        bp = b[:, HEAD_W * p:HEAD_W * (p + 1)]
        s_old = s_ref[p]
        s_bf = s_old.astype(BF16)
        qe = qp * jnp.exp(bp)
        b_end = bp[chunk - 1:chunk]
        kend_t = (kp * jnp.exp(b_end - bp)).T
        dec_t = jnp.broadcast_to(jnp.exp(b_end), (HEAD_W, HEAD_W)).T
        q_blk, k_blk = [], []
        for blk in range(n_sub):
            lo, hi = blk * sub, (blk + 1) * sub
            r = jnp.zeros((1, HEAD_W), F32) if blk == 0 else bp[lo - 1:lo]
            q_blk.append(qp[lo:hi] * jnp.exp(bp[lo:hi] - r))
            k_blk.append((kp * jnp.exp(jnp.where(krow < hi, r - bp, 0.0))).astype(BF16))
        upd, o_inter, att_raw, vbs = [], [], [], []
        for hh in range(2):
            h = 2 * p + hh
            vb = pg_ref[:, PG_V + GLA_DV * h:PG_V + GLA_DV * (h + 1)].astype(BF16)
            vbs.append(vb)
            o_inter.append(_dot(jnp.where(head_mask[hh], qe, 0.0).astype(BF16), s_bf))
            upd.append(_dot(kend_t[GLA_DK * hh:GLA_DK * (hh + 1)].astype(BF16), vb))
            att_raw.append([_nt_dot(jnp.where(head_mask[hh], q_blk[blk], 0.0).astype(BF16), k_blk[blk])
                            for blk in range(n_sub)])
        s_ref[p] = dec_t * s_old + jnp.concatenate(upd, axis=0)
        pair_work.append((o_inter, att_raw, vbs))

    for p in range(2):
        o_inter, att_raw, vbs = pair_work[p]
        for hh in range(2):
            h = 2 * p + hh
            att = [jnp.where(ci <= ri + blk * sub, att_raw[hh][blk], 0.0).astype(BF16) for blk in range(n_sub)]
            o = o_inter[hh] + jnp.concatenate([_dot(a, vbs[hh]) for a in att], axis=0)
            og_ref[:, GLA_DV * h:GLA_DV * (h + 1)] = _rms(o, gn_ref[...])

    @pl.when(step == pl.num_programs(0) - 1)
    def _():
        sfin_ref[...] = s_ref[...]


def _gla_prompt(pg, wg_pad, bg, gn, *, t_valid, chunk=128, sub=16):
    t_pad = pg.shape[0]
    n_steps = pl.cdiv(t_valid, chunk)
    assert n_steps * chunk <= t_pad
    og, sfin = pl.pallas_call(
        functools.partial(_gla_prompt_kernel, t_valid=t_valid, chunk=chunk, sub=sub),
        out_shape=[jax.ShapeDtypeStruct((n_steps * chunk, GLA_WIDTH), F32),
                   jax.ShapeDtypeStruct((2, HEAD_W, GLA_DV), F32)],
        grid=(n_steps,),
        in_specs=[pl.BlockSpec((chunk, PG_USED), lambda i: (i, 0)),
                  pl.BlockSpec((RANK_PAD, GLA_HEADS * GLA_DK), lambda i: (0, 0)),
                  pl.BlockSpec((1, GLA_HEADS * GLA_DK), lambda i: (0, 0)),
                  pl.BlockSpec((1, GLA_DV), lambda i: (0, 0))],
        out_specs=[pl.BlockSpec((chunk, GLA_WIDTH), lambda i: (i, 0)),
                   pl.BlockSpec((2, HEAD_W, GLA_DV), lambda i: (0, 0, 0))],
        scratch_shapes=[pltpu.VMEM((2, HEAD_W, GLA_DV), F32)],
        compiler_params=_cparams(("arbitrary",)),
        name="gla_prompt",
    )(pg, wg_pad, bg, gn)
    return og, sfin.reshape(GLA_HEADS, GLA_DK, GLA_DV)


def _gla_sample_kernel(pg_ref, st_ref, wg_ref, bg_ref, gn_ref, og_ref, snew_ref, *, n_tok):
    pad = HEAD_W - n_tok
    zpad = jnp.zeros((pad, HEAD_W), F32)
    gr = jnp.concatenate([pg_ref[0, :, PG_R:PG_R + RANK_PAD], jnp.zeros((pad, RANK_PAD), F32)], axis=0)
    x = _dot(gr.astype(BF16), wg_ref[...]) + bg_ref[...]
    rows = lax.broadcasted_iota(jnp.int32, (HEAD_W, 1), 0)
    lg = jnp.where(rows < n_tok, _log_sigmoid(x) * (1.0 / GLA_TAU), 0.0)
    b = _cumsum_rows(lg, HEAD_W)

    lane = lax.broadcasted_iota(jnp.int32, (1, HEAD_W), 1)
    head_mask = (lane < GLA_DK, lane >= GLA_DK)
    ri = lax.broadcasted_iota(jnp.int32, (HEAD_W, HEAD_W), 0)
    ci = lax.broadcasted_iota(jnp.int32, (HEAD_W, HEAD_W), 1)
    og = []
    for p in range(2):
        qp = jnp.concatenate([pg_ref[0, :,PG_Q + HEAD_W * p:PG_Q + HEAD_W * (p + 1)], zpad], axis=0) * (GLA_DK ** -0.5)
        kp = jnp.concatenate([pg_ref[0, :,PG_K + HEAD_W * p:PG_K + HEAD_W * (p + 1)], zpad], axis=0)
        bp = b[:, HEAD_W * p:HEAD_W * (p + 1)]
        s_old = jnp.concatenate([st_ref[0, 0, 2 * p], st_ref[0, 0, 2 * p + 1]], axis=0)
        s_bf = s_old.astype(BF16)
        qe = qp * jnp.exp(bp)
        kinv = (kp * jnp.exp(-bp)).astype(BF16)
        b_end = bp[HEAD_W - 1:HEAD_W]
        kend_t = (kp * jnp.exp(b_end - bp)).T
        dec_t = jnp.broadcast_to(jnp.exp(b_end), (HEAD_W, HEAD_W)).T
        upd = []
        for hh in range(2):
            h = 2 * p + hh
            vb = jnp.concatenate([pg_ref[0, :,PG_V + GLA_DV * h:PG_V + GLA_DV * (h + 1)], zpad], axis=0).astype(BF16)
            qm = jnp.where(head_mask[hh], qe, 0.0).astype(BF16)
            att = jnp.where(ci <= ri, _nt_dot(qm, kinv), 0.0)
            o = _dot(qm, s_bf) + _dot(att.astype(BF16), vb)
            og.append(_rms(o, gn_ref[...])[:n_tok])
            upd.append(_dot(kend_t[GLA_DK * hh:GLA_DK * (hh + 1)].astype(BF16), vb))
        s_new = dec_t * s_old + jnp.concatenate(upd, axis=0)
        snew_ref[0, 2 * p] = s_new[:GLA_DK]
        snew_ref[0, 2 * p + 1] = s_new[GLA_DK:]
    og_ref[0] = jnp.concatenate(og, axis=1)


def _gla_sample(pg, state, layer, wg_pad, bg, gn, *, n_seq, n_tok):
    pg3 = pg.reshape(n_seq, n_tok, PG_WIDTH)
    og, snew = pl.pallas_call(
        functools.partial(_gla_sample_kernel, n_tok=n_tok),
        out_shape=[jax.ShapeDtypeStruct((n_seq, n_tok, GLA_WIDTH), F32),
                   jax.ShapeDtypeStruct((n_seq, GLA_HEADS, GLA_DK, GLA_DV), F32)],
        grid=(n_seq,),
        in_specs=[pl.BlockSpec((1, n_tok, PG_USED), lambda i: (i, 0, 0)),
                  pl.BlockSpec((1, 1, GLA_HEADS, GLA_DK, GLA_DV), lambda i: (layer, i, 0, 0, 0)),
                  pl.BlockSpec((RANK_PAD, GLA_HEADS * GLA_DK), lambda i: (0, 0)),
                  pl.BlockSpec((1, GLA_HEADS * GLA_DK), lambda i: (0, 0)),
                  pl.BlockSpec((1, GLA_DV), lambda i: (0, 0))],
        out_specs=[pl.BlockSpec((1, n_tok, GLA_WIDTH), lambda i: (i, 0, 0)),
                   pl.BlockSpec((1, GLA_HEADS, GLA_DK, GLA_DV), lambda i: (i, 0, 0, 0))],
        compiler_params=_cparams(("arbitrary",)),
        name="gla_sample",
    )(pg3, state, wg_pad, bg, gn)
    return og.reshape(n_seq * n_tok, GLA_WIDTH), snew


def _diff_lambda(lqk_ref, lam_init):
    s1 = jnp.sum(lqk_ref[0:1, :] * lqk_ref[1:2, :], axis=-1, keepdims=True)
    s2 = jnp.sum(lqk_ref[2:3, :] * lqk_ref[3:4, :], axis=-1, keepdims=True)
    return jnp.exp(s1) - jnp.exp(s2) + lam_init


def _split_maps(q):
    lane = lax.broadcasted_iota(jnp.int32, q.shape, 1)
    zero = jnp.zeros_like(q)
    return jnp.concatenate([jnp.where(lane < DIFF_DH, q, zero), jnp.where(lane >= DIFF_DH, q, zero)], axis=0)


def _diff_prompt_kernel(lqk_ref, dn_ref, q_ref, k_ref, vt_ref, o_ref, m_ref, l_ref, acc_ref, s_ref, *, bq, bk,
                        lam_init):
    i = pl.program_id(1)
    qq = _split_maps(q_ref[0]) * (DIFF_DH ** -0.5)
    lane = lax.broadcasted_iota(jnp.int32, (1, 2 * bq), 1)
    qpos = i * bq + jnp.where(lane < bq, lane, lane - bq)
    m_ref[...] = jnp.full_like(m_ref, NEG)
    l_ref[...] = jnp.zeros_like(l_ref)
    acc_ref[...] = jnp.zeros_like(acc_ref)

    n_full = (i * bq + 1) // bk
    n_need = (i * bq + bq + bk - 1) // bk

    def scores(j):
        off = pl.multiple_of(j * bk, bk)
        return _nt_dot(k_ref[0, pl.ds(off, bk), :], qq)

    s_ref[...] = scores(0)

    def chunk(j, masked):
        s = s_ref[...]
        s_next = scores(jnp.minimum(j + 1, n_need - 1))
        if masked:
            kpos = j * bk + lax.broadcasted_iota(jnp.int32, (bk, 1), 0)
            s = jnp.where(kpos <= qpos, s, NEG)
        m_old = m_ref[...]
        m_new = jnp.maximum(m_old, jnp.max(s, axis=0, keepdims=True))
        alpha = jnp.exp(m_old - m_new)
        p = jnp.exp(s - m_new)
        l_ref[...] = alpha * l_ref[...] + jnp.sum(p, axis=0, keepdims=True)
        acc_ref[...] = alpha * acc_ref[...] + _dot(vt_ref[0, j], p.astype(BF16))
        m_ref[...] = m_new
        s_ref[...] = s_next

    def full_body(j, c):
        chunk(j, False)
        return c

    def diag_body(j, c):
        chunk(j, True)
        return c

    def pair_body(t, c):
        chunk(2 * t, False)
        chunk(2 * t + 1, False)
        return c

    n_pairs = n_full // 2
    lax.fori_loop(0, n_pairs, pair_body, 0)
    lax.fori_loop(2 * n_pairs, n_full, full_body, 0)
    lax.fori_loop(n_full, n_need, diag_body, 0)

    lam = _diff_lambda(lqk_ref, lam_init)
    on = acc_ref[...] / l_ref[...]
    o = (on[:, :bq] - lam * on[:, bq:]).T
    o_ref[...] = _rms(o, dn_ref[...]) * (1.0 - lam_init)


def _diff_prompt(q, k, vt, lqk, dn, *, t_valid, lam_init, bq=256):
    n_h, t_pad, _ = q.shape
    bk = vt.shape[-1]
    nq = pl.cdiv(t_valid, bq)
    assert pl.cdiv(nq * bq, bk) * bk <= t_pad
    return pl.pallas_call(
        functools.partial(_diff_prompt_kernel, bq=bq, bk=bk, lam_init=lam_init),
        out_shape=jax.ShapeDtypeStruct((nq * bq, n_h * HEAD_W), F32),
        grid=(n_h, nq),
        in_specs=[pl.BlockSpec((4, DIFF_DH), lambda h, i: (0, 0)),
                  pl.BlockSpec((1, HEAD_W), lambda h, i: (0, 0)),
                  pl.BlockSpec((1, bq, HEAD_W), lambda h, i: (h, i, 0)),
                  pl.BlockSpec((1, t_pad, HEAD_W), lambda h, i: (h, 0, 0)),
                  pl.BlockSpec((1, t_pad // bk, HEAD_W, bk), lambda h, i: (h, 0, 0, 0))],
        out_specs=pl.BlockSpec((bq, HEAD_W), lambda h, i: (i, h)),
        scratch_shapes=[pltpu.VMEM((1, 2 * bq), F32), pltpu.VMEM((1, 2 * bq), F32),
                        pltpu.VMEM((HEAD_W, 2 * bq), F32), pltpu.VMEM((bk, 2 * bq), F32)],
        compiler_params=_cparams(("arbitrary", "arbitrary")),
        name="diff_prompt",
    )(lqk, dn, q, k, vt)


def _bf16_round(x):
    return x.astype(BF16).astype(F32)


def _diff_sample_kernel(pt_ref, lqk_ref, dn_ref, q_ref, kn_ref, vn_ref, *refs, n_tok, pages_per_step, lam_init):
    g_pages = pages_per_step
    k_refs = refs[:g_pages]
    v_refs = refs[g_pages:2 * g_pages]
    o_ref = refs[2 * g_pages]
    qq_ref, m_ref, l_ref, acc_ref = refs[2 * g_pages + 1:]
    n_h = q_ref.shape[1]
    n2 = 2 * n_tok
    p = pl.program_id(1)

    @pl.when(p == 0)
    def _():
        qrow = lax.broadcasted_iota(jnp.int32, (n2, 1), 0) % n_tok
        for h in range(n_h):
            qq = _split_maps(q_ref[0, h]) * (DIFF_DH ** -0.5)
            qq_ref[h] = qq.astype(F32)
            qf = qq.astype(F32)
            kn = _bf16_round(kn_ref[0, h])
            vn = _bf16_round(vn_ref[0, h])
            s = [jnp.where(qrow >= t, jnp.sum(qf * kn[t:t + 1], axis=-1, keepdims=True), NEG) for t in range(n_tok)]
            m = s[0]
            for t in range(1, n_tok):
                m = jnp.maximum(m, s[t])
            l = jnp.zeros((n2, 1), F32)
            acc = jnp.zeros((n2, HEAD_W), F32)
            for t in range(n_tok):
                pt = jnp.exp(s[t] - m)
                l = l + pt
                acc = acc + _bf16_round(pt) * vn[t:t + 1]
            m_ref[h] = m
            l_ref[h] = l
            acc_ref[h] = acc

    page = k_refs[0].shape[3]
    scores = []
    for h in range(n_h):
        qh = qq_ref[h].astype(BF16)
        scores.append(jnp.concatenate(
            [_nt_dot(qh, k_refs[g][0, 0, h].astype(BF16)) for g in range(g_pages)], axis=1))
    probs, alphas = [], []
    for h in range(n_h):
        s = scores[h]
        m_old = m_ref[h]
        m_new = jnp.maximum(m_old, jnp.max(s, axis=-1, keepdims=True))
        alpha = jnp.exp(m_old - m_new)
        p32 = jnp.exp(s - m_new)
        l_ref[h] = alpha * l_ref[h] + jnp.sum(p32, axis=-1, keepdims=True)
        m_ref[h] = m_new
        probs.append(p32.astype(BF16))
        alphas.append(alpha)
    for h in range(n_h):
        pv = _dot(probs[h][:, :page], v_refs[0][0, 0, h].astype(BF16))
        for g in range(1, g_pages):
            pv += _dot(probs[h][:, g * page:(g + 1) * page], v_refs[g][0, 0, h].astype(BF16))
        acc_ref[h] = alphas[h] * acc_ref[h] + pv

    @pl.when(p == pl.num_programs(1) - 1)
    def _():
        lam = _diff_lambda(lqk_ref, lam_init)
        outs = []
        for h in range(n_h):
            on = acc_ref[h] / l_ref[h]
            o = on[:n_tok] - lam * on[n_tok:]
            outs.append(_rms(o, dn_ref[...]) * (1.0 - lam_init))
        o_ref[0] = jnp.concatenate(outs, axis=1)


def _page_specs(layer, n_pages, pages_per_step, n_h, page, reverse):
    specs = []
    for g in range(pages_per_step):
        def imap(b, p, pt, g=g):
            idx = p * pages_per_step + g
            if reverse:
                idx = n_pages - 1 - idx
            return (layer, pt[b * n_pages + idx], 0, 0, 0)
        specs.append(pl.BlockSpec((1, 1, n_h, page, HEAD_W), imap))
    return specs


def _diff_sample(q, kn, vn, cache_k, cache_v, page_table, layer, lqk, dn, *, lam_init, pages_per_step=PAGES_PER_STEP):
    n_seq, n_h, n_tok, _ = q.shape
    n_pages = page_table.shape[1]
    page = cache_k.shape[3]
    assert n_pages % pages_per_step == 0
    tok_spec = pl.BlockSpec((1, n_h, n_tok, HEAD_W), lambda b, p, pt: (b, 0, 0, 0))
    pspecs = _page_specs(layer, n_pages, pages_per_step, n_h, page, reverse=False)
    out = pl.pallas_call(
        functools.partial(_diff_sample_kernel, n_tok=n_tok, pages_per_step=pages_per_step, lam_init=lam_init),
        out_shape=jax.ShapeDtypeStruct((n_seq, n_tok, n_h * HEAD_W), F32),
        grid_spec=pltpu.PrefetchScalarGridSpec(
            num_scalar_prefetch=1,
            grid=(n_seq, n_pages // pages_per_step),
            in_specs=[pl.BlockSpec((4, DIFF_DH), lambda b, p, pt: (0, 0)),
                      pl.BlockSpec((1, HEAD_W), lambda b, p, pt: (0, 0)),
                      tok_spec, tok_spec, tok_spec] + pspecs + pspecs,
            out_specs=pl.BlockSpec((1, n_tok, n_h * HEAD_W), lambda b, p, pt: (b, 0, 0)),
            scratch_shapes=[pltpu.VMEM((n_h, 2 * n_tok, HEAD_W), F32), pltpu.VMEM((n_h, 2 * n_tok, 1), F32),
                            pltpu.VMEM((n_h, 2 * n_tok, 1), F32), pltpu.VMEM((n_h, 2 * n_tok, HEAD_W), F32)],
        ),
        compiler_params=_cparams(("arbitrary", "arbitrary")),
        name="diff_sample",
    )(page_table.reshape(-1), lqk, dn, q, kn, vn, *([cache_k] * pages_per_step), *([cache_v] * pages_per_step))
    return out.reshape(n_seq * n_tok, n_h * HEAD_W)


def _later_keys_matrix(n):
    ri = lax.broadcasted_iota(jnp.int32, (n, n), 0)
    ci = lax.broadcasted_iota(jnp.int32, (n, n), 1)
    return jnp.where(ri > ci, 1.0, 0.0).astype(BF16)


def _tail_sums(lk, u):
    n = lk.shape[0]
    hi = lk.astype(BF16)
    lo = (lk - hi.astype(F32)).astype(BF16)
    t = _dot(jnp.concatenate([hi, lo], axis=0), u)
    return t[:n] + t[n:]


def _sb_prompt_kernel(q_ref, k_ref, vt_ref, o_ref, c_ref, acc_ref, *, blk, heads):
    i = pl.program_id(1)
    ri = lax.broadcasted_iota(jnp.int32, (blk, blk), 0)
    ci = lax.broadcasted_iota(jnp.int32, (blk, blk), 1)
    later = jnp.where(ci > ri, 1.0, 0.0).astype(BF16)
    qpos = i * blk + lax.broadcasted_iota(jnp.int32, (1, blk), 1)
    c_ref[...] = jnp.zeros_like(c_ref)
    acc_ref[...] = jnp.zeros_like(acc_ref)

    def chunk(j, masked):
        off = pl.multiple_of(j * blk, blk)
        zs = [_nt_dot(k_ref[h, pl.ds(off, blk), :], q_ref[h]) * (SB_DH ** -0.5) for h in range(heads)]
        if masked:
            kpos = j * blk + lax.broadcasted_iota(jnp.int32, (blk, 1), 0)
            vis = kpos < qpos
        lbs, lks, tails = [], [], []
        for h in range(heads):
            sp = _softplus(zs[h])
            lk = -sp
            lbs.append(zs[h] - sp)
            if masked:
                lk = jnp.where(vis, lk, 0.0)
            lks.append(lk)
            hi = lk.astype(BF16)
            lo = (lk - hi.astype(F32)).astype(BF16)
            t = _dot(later, jnp.concatenate([hi, lo], axis=1))
            tails.append(t[:, :blk] + t[:, blk:])
        c_max = None
        for h in range(heads):
            c = c_ref[h]
            a = jnp.exp(lbs[h] + tails[h] + c)
            if masked:
                a = jnp.where(vis, a, 0.0)
            acc_ref[h] += _dot(vt_ref[h, j], a.astype(BF16))
            c_new = c + jnp.sum(lks[h], axis=0, keepdims=True)
            c_ref[h] = c_new
            m = jnp.max(c_new)
            c_max = m if c_max is None else jnp.maximum(c_max, m)
        return c_max

    def cond(carry):
        t, c_max = carry
        return jnp.logical_and(t < i, c_max > EXP_UNDERFLOW)

    def body(carry):
        t, _ = carry
        return t + 1, chunk(i - 1 - t, False)

    lax.while_loop(cond, body, (jnp.int32(0), chunk(i, True)))
    for h in range(heads):
        o_ref[:, HEAD_W * h:HEAD_W * (h + 1)] = acc_ref[h].T


def _sb_prompt(q, k, vt, *, t_valid, heads=SB_HEADS_PER_STEP):
    n_h, t_pad, _ = q.shape
    blk = vt.shape[-1]
    nq = pl.cdiv(t_valid, blk)
    assert nq * blk <= t_pad and n_h % heads == 0
    return pl.pallas_call(
        functools.partial(_sb_prompt_kernel, blk=blk, heads=heads),
        out_shape=jax.ShapeDtypeStruct((nq * blk, n_h * HEAD_W), F32),
        grid=(n_h // heads, nq),
        in_specs=[pl.BlockSpec((heads, blk, HEAD_W), lambda h, i: (h, i, 0)),
                  pl.BlockSpec((heads, t_pad, HEAD_W), lambda h, i: (h, 0, 0)),
                  pl.BlockSpec((heads, t_pad // blk, HEAD_W, blk), lambda h, i: (h, 0, 0, 0))],
        out_specs=pl.BlockSpec((blk, heads * HEAD_W), lambda h, i: (i, h)),
        scratch_shapes=[pltpu.VMEM((heads, 1, blk), F32), pltpu.VMEM((heads, HEAD_W, blk), F32)],
        compiler_params=_cparams(("arbitrary", "arbitrary")),
        name="sb_prompt",
    )(q, k, vt)


def _sb_new_tokens(q8, kn, vn, n_tok):
    rows8 = q8.shape[0]
    qrow = lax.broadcasted_iota(jnp.int32, (rows8, 1), 0)
    c = jnp.zeros((rows8, 1), F32)
    acc = jnp.zeros((rows8, HEAD_W), F32)
    for t in range(n_tok - 1, -1, -1):
        z = jnp.sum(q8 * kn[t:t + 1], axis=-1, keepdims=True) * (SB_DH ** -0.5)
        sp = _softplus(z)
        vis = qrow > t
        a = jnp.where(vis, jnp.exp(z - sp + c), 0.0)
        acc = acc + _bf16_round(a) * vn[t:t + 1]
        c = c + jnp.where(vis, -sp, 0.0)
    return c, acc


def _sb_sample_kernel(pt_ref, done_ref, q_ref, *refs, n_tok, pages_per_step, first_stage):
    g_pages = pages_per_step
    seed_a, seed_b = refs[:2]
    k_refs = refs[2:2 + g_pages]
    v_refs = refs[2 + g_pages:2 + 2 * g_pages]
    n_out = 3 if first_stage else 1
    out_refs = refs[2 + 2 * g_pages:2 + 2 * g_pages + n_out]
    c_ref, acc_ref = refs[2 + 2 * g_pages + n_out:]
    n_h = q_ref.shape[1]
    page = k_refs[0].shape[3]
    rows8 = 8
    b = pl.program_id(0)
    p = pl.program_id(1)
    q8s = [jnp.concatenate([q_ref[0, h].astype(F32), jnp.zeros((rows8 - n_tok, HEAD_W), F32)], axis=0)
           for h in range(n_h)]

    @pl.when(p == 0)
    def _():
        for h in range(n_h):
            if first_stage:
                c, acc = _sb_new_tokens(q8s[h], _bf16_round(seed_a[0, h]), _bf16_round(seed_b[0, h]), n_tok)
            else:
                c, acc = seed_a[0, h], seed_b[0, h]
            c_ref[h] = c
            acc_ref[h] = acc

    def visit_pages():
        _sb_visit_pages(q8s, k_refs, v_refs, c_ref, acc_ref, page)

    if first_stage:
        visit_pages()
    else:
        pl.when(done_ref[b] == 0)(visit_pages)

    @pl.when(p == pl.num_programs(1) - 1)
    def _():
        if first_stage:
            c_out, acc_out, done_out = out_refs
            c_max = jnp.full((1, 1), NEG, F32)
            for h in range(n_h):
                c_out[0, h] = c_ref[h]
                acc_out[0, h] = acc_ref[h]
                c_max = jnp.maximum(c_max, jnp.max(c_ref[h][:n_tok], axis=0, keepdims=True))
            done_out[0] = jnp.where(c_max < EXP_UNDERFLOW, 1, 0).astype(jnp.int32)
        else:
            out_refs[0][0] = jnp.concatenate([acc_ref[h][:n_tok] for h in range(n_h)], axis=1)


def _sb_visit_pages(q8s, k_refs, v_refs, c_ref, acc_ref, page):
    g_pages = len(k_refs)
    n_h = len(q8s)
    rows8 = q8s[0].shape[0]
    u = _later_keys_matrix(page)
    zs = []
    for h in range(n_h):
        qh = q8s[h].astype(BF16)
        zs.append(jnp.concatenate(
            [_nt_dot(qh, k_refs[g][0, 0, h].astype(BF16)) for g in range(g_pages)], axis=0) * (SB_DH ** -0.5))
    lbs, tails, psums = [], [], []
    for h in range(n_h):
        sp = _softplus(zs[h])
        lbs.append(zs[h] - sp)
        tails.append(_tail_sums(-sp, u))
        psums.append(jnp.sum(-sp, axis=-1, keepdims=True))
    for h in range(n_h):
        c = c_ref[h]
        cs = []
        for g in range(g_pages):
            cs.append(c)
            c = c + psums[h][rows8 * g:rows8 * (g + 1)]
        c_ref[h] = c
        a = jnp.exp(lbs[h] + tails[h] + jnp.concatenate(cs, axis=0))
        pv = _dot(a[:rows8].astype(BF16), v_refs[0][0, 0, h].astype(BF16))
        for g in range(1, g_pages):
            pv += _dot(a[rows8 * g:rows8 * (g + 1)].astype(BF16), v_refs[g][0, 0, h].astype(BF16))
        acc_ref[h] += pv


def _sb_sample(q, kn, vn, cache_k, cache_v, page_table, layer, *, pages_per_step=PAGES_PER_STEP):
    n_seq, n_h, n_tok, _ = q.shape
    n_pages = page_table.shape[1]
    page = cache_k.shape[3]
    g_pages = pages_per_step
    assert n_pages % g_pages == 0 and n_pages > g_pages and n_tok <= 8
    pt = page_table.reshape(-1)
    tok_spec = pl.BlockSpec((1, n_h, n_tok, HEAD_W), lambda b, p, pt, done: (b, 0, 0, 0))
    c_spec = pl.BlockSpec((1, n_h, 8, 1), lambda b, p, pt, done: (b, 0, 0, 0))
    acc_spec = pl.BlockSpec((1, n_h, 8, HEAD_W), lambda b, p, pt, done: (b, 0, 0, 0))
    scratch = [pltpu.VMEM((n_h, 8, 1), F32), pltpu.VMEM((n_h, 8, HEAD_W), F32)]

    def page_specs(first_idx, skip_done, per_step):
        specs = []
        for g in range(per_step):
            def imap(b, p, pt, done, g=g):
                phys = pt[b * n_pages + (n_pages - 1 - (first_idx + p * per_step + g))]
                if skip_done:
                    phys = jnp.where(done[b] == 0, phys, 0)
                return (layer, phys, 0, 0, 0)
            specs.append(pl.BlockSpec((1, 1, n_h, page, HEAD_W), imap))
        return specs

    specs1 = page_specs(0, False, g_pages)
    c1, acc1, done = pl.pallas_call(
        functools.partial(_sb_sample_kernel, n_tok=n_tok, pages_per_step=g_pages, first_stage=True),
        out_shape=[jax.ShapeDtypeStruct((n_seq, n_h, 8, 1), F32),
                   jax.ShapeDtypeStruct((n_seq, n_h, 8, HEAD_W), F32),
                   jax.ShapeDtypeStruct((n_seq, 1, 1), jnp.int32)],
        grid_spec=pltpu.PrefetchScalarGridSpec(
            num_scalar_prefetch=2,
            grid=(n_seq, 1),
            in_specs=[tok_spec, tok_spec, tok_spec] + specs1 + specs1,
            out_specs=[c_spec, acc_spec, pl.BlockSpec((1, 1, 1), lambda b, p, pt, done: (b, 0, 0))],
            scratch_shapes=scratch,
        ),
        compiler_params=_cparams(("arbitrary", "arbitrary")),
        name="sb_sample_newest",
    )(pt, jnp.zeros((n_seq,), jnp.int32), q, kn, vn, *([cache_k] * g_pages), *([cache_v] * g_pages))

    n_rest = n_pages - g_pages
    g_rest = max(g for g in range(1, MAX_REST_PAGES_PER_STEP + 1) if n_rest % g == 0)
    specs2 = page_specs(g_pages, True, g_rest)
    rest_call = pl.pallas_call(
        functools.partial(_sb_sample_kernel, n_tok=n_tok, pages_per_step=g_rest, first_stage=False),
        out_shape=jax.ShapeDtypeStruct((n_seq, n_tok, n_h * HEAD_W), F32),
        grid_spec=pltpu.PrefetchScalarGridSpec(
            num_scalar_prefetch=2,
            grid=(n_seq, n_rest // g_rest),
            in_specs=[tok_spec, c_spec, acc_spec] + specs2 + specs2,
            out_specs=pl.BlockSpec((1, n_tok, n_h * HEAD_W), lambda b, p, pt, done: (b, 0, 0)),
            scratch_shapes=scratch,
        ),
        compiler_params=_cparams(("arbitrary", "arbitrary")),
        name="sb_sample_rest",
    )

    def visit_rest():
        return rest_call(pt, done.reshape(-1), q, c1, acc1, *([cache_k] * g_rest), *([cache_v] * g_rest))

    def nothing_left():
        return acc1[:, :, :n_tok, :].transpose(0, 2, 1, 3).reshape(n_seq, n_tok, n_h * HEAD_W)

    out = lax.cond(jnp.all(done == 1), nothing_left, visit_rest)
    return out.reshape(n_seq * n_tok, n_h * HEAD_W)


def _merge_kernel(x_ref, og_ref, gg_ref, od_ref, dg_ref, os_ref, sg_ref, w_ref, fg_ref, o_ref, *, final_norm):
    ag = (og_ref[...] * _silu(gg_ref[...])).astype(BF16)
    ad = (od_ref[...] * _silu(dg_ref[...])).astype(BF16)
    asb = (os_ref[...] * _silu(sg_ref[...])).astype(BF16)
    y = _dot(ag, w_ref[0:GLA_WIDTH, :])
    y += _dot(ad, w_ref[GLA_WIDTH:GLA_WIDTH + ATT_WIDTH, :])
    y += _dot(asb, w_ref[GLA_WIDTH + ATT_WIDTH:, :])
    out = x_ref[...] + y
    o_ref[...] = _rms(out, fg_ref[...]) if final_norm else out


def _merge(x, og, pg, od, dg, osb, sg, w_out, final_g, *, tm, final_norm):
    t_valid, d = x.shape
    assert PG_G % GLA_WIDTH == 0
    return pl.pallas_call(
        functools.partial(_merge_kernel, final_norm=final_norm),
        out_shape=jax.ShapeDtypeStruct((t_valid, d), F32),
        grid=(pl.cdiv(t_valid, tm),),
        in_specs=[pl.BlockSpec((tm, d), lambda i: (i, 0)),
                  pl.BlockSpec((tm, GLA_WIDTH), lambda i: (i, 0)),
                  pl.BlockSpec((tm, GLA_WIDTH), lambda i: (i, PG_G // GLA_WIDTH)),
                  pl.BlockSpec((tm, ATT_WIDTH), lambda i: (i, 0)),
                  pl.BlockSpec((tm, ATT_WIDTH), lambda i: (i, 0)),
                  pl.BlockSpec((tm, ATT_WIDTH), lambda i: (i, 0)),
                  pl.BlockSpec((tm, ATT_WIDTH), lambda i: (i, 0)),
                  pl.BlockSpec(w_out.shape, lambda i: (0, 0)),
                  pl.BlockSpec((1, d), lambda i: (0, 0))],
        out_specs=pl.BlockSpec((tm, d), lambda i: (i, 0)),
        compiler_params=_cparams(("arbitrary",)),
        name="merge",
    )(x, og, pg, od, dg, osb, sg, w_out, final_g.reshape(1, d))


def _head_major_to_cache(a):
    return jnp.transpose(a, (0, 2, 1, 3))


def kernel(x_prompt, x_sample, cache_diff_k, cache_diff_v, cache_sb_k, cache_sb_v, state_gla, page_table, meta_tokens, norm_mix, w_in, gla_w_gate, gla_b_gate, gla_norm, diff_lambda_q1, diff_lambda_k1, diff_lambda_q2, diff_lambda_k2, diff_norm, w_out, final_norm):
    n_batch, seq, d = x_prompt.shape
    assert n_batch == 1
    n_seq, n_tok, _ = x_sample.shape
    depth = w_in.shape[0]
    t_valid = seq + META_LEN
    row_tile = 512
    t_pad = pl.cdiv(t_valid, row_tile) * row_tile
    n_s = n_seq * n_tok

    xp = jnp.concatenate([meta_tokens.astype(x_prompt.dtype), x_prompt[0]], axis=0)
    xs = x_sample.reshape(n_s, d)
    caches = [jnp.transpose(c, (0, 1, 3, 2, 4)) for c in (cache_diff_k, cache_diff_v, cache_sb_k, cache_sb_v)]

    p_rows, s_rows = [], []
    p_slabs = None
    for l in range(depth):
        lam_init = 0.8 - 0.6 * math.exp(-0.3 * l)
        last = l == depth - 1
        w_packed = _pack_w_in(w_in[l])
        w_out_bf = w_out[l].astype(BF16)
        wg_pad = jnp.zeros((RANK_PAD, GLA_HEADS * GLA_DK), F32).at[:GLA_RANK].set(gla_w_gate[l]).astype(BF16)
        bg = gla_b_gate[l].reshape(1, -1)
        gn = gla_norm[l].reshape(1, -1)
        dn = diff_norm[l].reshape(1, -1)
        lqk = jnp.stack([diff_lambda_q1[l], diff_lambda_k1[l], diff_lambda_q2[l], diff_lambda_k2[l]])

        pr = _inproj_all(xp, norm_mix[l], w_packed, t_pad=t_pad, tm=row_tile,
                         value_chunks={"d": DIFF_KEY_CHUNK, "s": SB_KEY_CHUNK}, layer_slabs=(l, depth, p_slabs))
        p_slabs = {k: pr[k] for k in ("dk", "dv", "sk", "sv")}
        og, s_fin = _gla_prompt(pr["pg"], wg_pad, bg, gn, t_valid=t_valid)
        od = _diff_prompt(pr["dq"], pr["dkb"], pr["dvt"], lqk, dn, t_valid=t_valid, lam_init=lam_init)
        osb = _sb_prompt(pr["sq"], pr["skb"], pr["svt"], t_valid=t_valid)
        xp = _merge(xp, og, pr["pg"], od, pr["dg"], osb, pr["sg"], w_out_bf, final_norm, tm=row_tile,
                    final_norm=last)
        p_rows.append(s_fin[None])

        sr = _inproj_all(xs, norm_mix[l], w_packed, t_pad=n_s, tm=n_s, value_chunks=None)
        og_s, s_new = _gla_sample(sr["pg"], state_gla, l, wg_pad, bg, gn, n_seq=n_seq, n_tok=n_tok)
        per_seq = lambda a: a.reshape(a.shape[0], n_seq, n_tok, HEAD_W).transpose(1, 0, 2, 3)
        od_s = _diff_sample(per_seq(sr["dq"]), per_seq(sr["dk"]), per_seq(sr["dv"]), caches[0], caches[1],
                            page_table, l, lqk, dn, lam_init=lam_init)
        os_s = _sb_sample(per_seq(sr["sq"]), per_seq(sr["sk"]), per_seq(sr["sv"]), caches[2], caches[3],
                          page_table, l)
        xs = _merge(xs, og_s, sr["pg"], od_s, sr["dg"], os_s, sr["sg"], w_out_bf, final_norm, tm=n_s,
                    final_norm=last)
        to_cache = lambda a: a.reshape(a.shape[0], n_seq, n_tok, HEAD_W).transpose(1, 2, 0, 3)
        s_rows.append(tuple(to_cache(sr[k]) for k in ("dk", "dv", "sk", "sv")) + (s_new,))

    y_prompt = xp[META_LEN:][None]
    y_sample = xs.reshape(n_seq, n_tok, d)
    stack = lambda rows, i: jnp.stack([r[i] for r in rows])
    p_cache = [_head_major_to_cache(p_slabs[k])[:, None] for k in ("dk", "dv", "sk", "sv")]
    return (y_prompt, y_sample,
            p_cache[0], p_cache[1], p_cache[2], p_cache[3], jnp.stack(p_rows),
            stack(s_rows, 0), stack(s_rows, 1), stack(s_rows, 2), stack(s_rows, 3), stack(s_rows, 4))
```
